```python
import math
import jax, jax.numpy as jnp
from jax import lax
import numpy as np

D_MODEL = 2048
BATCH = 32
SEQ = 256
DEPTH = 4
DEC_BATCH = 2
DEC_SEQ = 2048
PAST_LEN = 512

GRID_W = 64
HEAD_DIM = 128
N_HEADS = 8
N_KV_HEADS = 2
KV_GROUP = N_HEADS // N_KV_HEADS
D_ATTN = N_HEADS * HEAD_DIM
D_KV = N_KV_HEADS * HEAD_DIM
WINDOW = 128
Q_BLOCK = 128
ROPE_THETA = 10000.0
D_LRU = D_MODEL // 4
LRU_BLOCKS = 4
LRU_BLK = D_LRU // LRU_BLOCKS
CONV_W = 4
CONV_LEFT = 2
LRU_C = 8.0
D_POOL = D_MODEL // 4
POOL_WINDOWS = (2, 4, 8, 16)
N_POOL_GROUPS = len(POOL_WINDOWS)
POOL_GROUP = D_POOL // N_POOL_GROUPS
D_MIX = D_ATTN + D_LRU + D_POOL
D_IN = D_ATTN + 2 * D_KV + 2 * D_LRU + D_POOL
SPLITS = (D_ATTN, D_ATTN + D_KV, D_ATTN + 2 * D_KV, D_ATTN + 2 * D_KV + D_LRU, D_ATTN + 2 * D_KV + 2 * D_LRU)
D_FF = -(-(8 * D_MODEL) // (3 * 256)) * 256
N_MOD = 6
RMS_EPS = 1e-6
MOD_INIT = 0.5
NEG_INF = -1e30

kernel_name = 'hymba_diffusion_prefix_step'


def _rmsnorm(x, g):
    xf = x.astype(jnp.float32)
    y = xf * lax.rsqrt(jnp.mean(xf * xf, axis=-1, keepdims=True) + RMS_EPS)
    return (y * g.astype(jnp.float32)).astype(x.dtype)


def _mod_vectors(cond, w, b):
    m = jax.nn.silu(cond) @ w + b
    m = m.reshape(cond.shape[0], N_MOD, D_MODEL)
    return jnp.moveaxis(m, 1, 0)[:, :, None, :]


def _rope_2d_tables(T, dtype):
    rows = T // GRID_W
    row = jnp.repeat(jnp.arange(rows), GRID_W)
    col = jnp.tile(jnp.arange(GRID_W), rows)
    pos = jnp.stack([row, col], axis=-1).astype(jnp.float32)
    rd = HEAD_DIM // 4
    inv = ROPE_THETA ** (-jnp.arange(rd, dtype=jnp.float32) / rd)
    ang = jnp.broadcast_to(pos[:, :, None, None] * inv, (T, 2, 2, rd)).reshape(T, HEAD_DIM)
    return jnp.cos(ang).astype(dtype), jnp.sin(ang).astype(dtype)


def _apply_rope(x, cos, sin):
    xr = x.reshape(x.shape[:-1] + (2, 2, HEAD_DIM // 4))
    rot = jnp.stack([-xr[..., 1, :], xr[..., 0, :]], axis=-2).reshape(x.shape)
    return x * cos[None, :, None, :] + rot * sin[None, :, None, :]


def _attend(q, k, v, bias, sink):
    s = jnp.einsum('bqkgd,bskd->bkgqs', q, k).astype(jnp.float32) * (HEAD_DIM ** -0.5) + bias
    B, Q = q.shape[0], q.shape[1]
    sink_col = jnp.broadcast_to(sink.astype(jnp.float32)[None, :, :, None, None], (B, N_KV_HEADS, KV_GROUP, Q, 1))
    p = jax.nn.softmax(jnp.concatenate([s, sink_col], axis=-1), axis=-1)[..., :-1]
    return jnp.einsum('bkgqs,bskd->bqkgd', p.astype(v.dtype), v)


def _context_attention(q, k, v, sink):
    B, C = q.shape[:2]
    bias = jnp.zeros((Q_BLOCK, C), jnp.float32)

    def blk(i):
        qb = lax.dynamic_slice_in_dim(q, i * Q_BLOCK, Q_BLOCK, axis=1)
        return _attend(qb, k, v, bias, sink)

    o = lax.map(blk, jnp.arange(C // Q_BLOCK))
    return jnp.moveaxis(o, 0, 1).reshape(B, C, D_ATTN)


def _latent_attention(q, k, v, ctx_k, ctx_v, sink):
    B, T = q.shape[:2]
    C = ctx_k.shape[1]
    band = Q_BLOCK + 2 * WINDOW
    kp = jnp.pad(k, ((0, 0), (WINDOW, WINDOW), (0, 0), (0, 0)))
    vp = jnp.pad(v, ((0, 0), (WINDOW, WINDOW), (0, 0), (0, 0)))
    rel = jnp.arange(Q_BLOCK)[:, None] - jnp.arange(band)[None, :] + WINDOW
    ctx_bias = jnp.zeros((Q_BLOCK, C), jnp.float32)

    def blk(i):
        start = i * Q_BLOCK
        qb = lax.dynamic_slice_in_dim(q, start, Q_BLOCK, axis=1)
        kb = lax.dynamic_slice_in_dim(kp, start, band, axis=1)
        vb = lax.dynamic_slice_in_dim(vp, start, band, axis=1)
        kpos = start - WINDOW + jnp.arange(band)
        valid = (jnp.abs(rel) <= WINDOW) & ((kpos >= 0) & (kpos < T))[None, :]
        bias = jnp.concatenate([jnp.where(valid, 0.0, NEG_INF).astype(jnp.float32), ctx_bias], axis=1)
        return _attend(qb, jnp.concatenate([kb, ctx_k], axis=1), jnp.concatenate([vb, ctx_v], axis=1), bias, sink)

    o = lax.map(blk, jnp.arange(T // Q_BLOCK))
    return jnp.moveaxis(o, 0, 1).reshape(B, T, D_ATTN)


def _dwconv(u, w, b):
    T = u.shape[1]
    up = jnp.pad(u, ((0, 0), (CONV_LEFT, CONV_W - 1 - CONV_LEFT), (0, 0)))
    return sum(up[:, j:j + T] * w[j] for j in range(CONV_W)) + b


def _lin_combine(left, right):
    a1, b1 = left
    a2, b2 = right
    return a1 * a2, a2 * b1 + b2


def _rglru_bidir(u, wa, ba, wx, bx, lam, h0):
    B, T, _ = u.shape
    ub = u.reshape(B, T, LRU_BLOCKS, LRU_BLK)
    uf = u.astype(jnp.float32)
    y = jnp.zeros((B, T, D_LRU), jnp.float32)
    finals = []
    for d in range(2):
        r = jax.nn.sigmoid((jnp.einsum('btnc,ncd->btnd', ub, wa[d]).reshape(B, T, D_LRU) + ba[d]).astype(jnp.float32))
        ig = jax.nn.sigmoid((jnp.einsum('btnc,ncd->btnd', ub, wx[d]).reshape(B, T, D_LRU) + bx[d]).astype(jnp.float32))
        log_a = LRU_C * r * jax.nn.log_sigmoid(lam[d].astype(jnp.float32))
        a = jnp.exp(log_a)
        bterm = jnp.sqrt(-jnp.expm1(2.0 * log_a)) * ig * uf
        if d == 1:
            a, bterm = a[:, ::-1], bterm[:, ::-1]
        a_cum, b_cum = lax.associative_scan(_lin_combine, (a, bterm), axis=1)
        h = a_cum * h0[:, d, None, :].astype(jnp.float32) + b_cum
        finals.append(h[:, -1])
        if d == 1:
            h = h[:, ::-1]
        y = y + h
    return y.astype(u.dtype), jnp.stack(finals, axis=1).astype(u.dtype)


def _pool_mixer(u, w, scale):
    B, T, _ = u.shape
    ug = u.reshape(B, T, N_POOL_GROUPS, POOL_GROUP).astype(jnp.float32)
    cs = jnp.pad(jnp.cumsum(ug, axis=1), ((0, 0), (1, 0), (0, 0), (0, 0)))
    t = jnp.arange(T)
    groups = []
    for g, win in enumerate(POOL_WINDOWS):
        lo = jnp.clip(t - win // 2, 0, T)
        hi = jnp.clip(t + win // 2, 0, T)
        cs_g = cs[:, :, g]
        mean = (cs_g[:, hi] - cs_g[:, lo]) / (hi - lo).astype(jnp.float32)[None, :, None]
        groups.append(mean - ug[:, :, g])
    pooled = jnp.stack(groups, axis=2).astype(u.dtype)
    y = jnp.einsum('btgc,gcd->btgd', pooled, w).reshape(B, T, D_POOL)
    return y * scale


def _mixer_inputs(x, mod, p):
    h = _rmsnorm(x, p['norm_mix']) * (1 + mod[1]) + mod[0]
    return jnp.split(h @ p['w_in'], SPLITS, axis=-1)


def _layer_tail(x, mod, attn, lru_h, lg, pool, p):
    mix = jnp.concatenate([attn, lru_h * jax.nn.gelu(lg), pool], axis=-1) @ p['w_out']
    x = x + mod[2] * mix
    h = _rmsnorm(x, p['norm_ffn']) * (1 + mod[4]) + mod[3]
    ga, up = jnp.split(h @ p['ffn_w1'], 2, axis=-1)
    return x + mod[5] * ((jax.nn.silu(ga) * up) @ p['ffn_w2'])


def _context_layer(x, mod, p):
    B, C, _ = x.shape
    q, k, v, lx, lg, pu = _mixer_inputs(x, mod, p)
    q = q.reshape(B, C, N_KV_HEADS, KV_GROUP, HEAD_DIM)
    k = k.reshape(B, C, N_KV_HEADS, HEAD_DIM)
    v = v.reshape(B, C, N_KV_HEADS, HEAD_DIM)
    attn = _context_attention(q, k, v, p['sink'])
    u = _dwconv(lx, p['conv_w'], p['conv_b'])
    h0 = jnp.zeros((B, 2, D_LRU), x.dtype)
    lru_h, h_fin = _rglru_bidir(u, p['lru_wa'], p['lru_ba'], p['lru_wx'], p['lru_bx'], p['lru_lambda'], h0)
    pool = _pool_mixer(pu, p['pool_w'], p['pool_scale'])
    return _layer_tail(x, mod, attn, lru_h, lg, pool, p), k, v, h_fin


def _latent_layer(x, mod, ctx_k, ctx_v, ctx_h, p):
    B, T, _ = x.shape
    q, k, v, lx, lg, pu = _mixer_inputs(x, mod, p)
    cos, sin = _rope_2d_tables(T, x.dtype)
    q = _apply_rope(q.reshape(B, T, N_HEADS, HEAD_DIM), cos, sin).reshape(B, T, N_KV_HEADS, KV_GROUP, HEAD_DIM)
    k = _apply_rope(k.reshape(B, T, N_KV_HEADS, HEAD_DIM), cos, sin)
    v = v.reshape(B, T, N_KV_HEADS, HEAD_DIM)
    attn = _latent_attention(q, k, v, ctx_k, ctx_v, p['sink'])
    u = _dwconv(lx, p['conv_w'], p['conv_b'])
    lru_h, _ = _rglru_bidir(u, p['lru_wa'], p['lru_ba'], p['lru_wx'], p['lru_bx'], p['lru_lambda'], ctx_h)
    pool = _pool_mixer(pu, p['pool_w'], p['pool_scale'])
    return _layer_tail(x, mod, attn, lru_h, lg, pool, p)


def setup_inputs(seed: int = 0) -> dict:
    key = jax.random.key(seed)
    ks = jax.random.split(key, 32)
    f32 = jnp.float32

    def nrm(k, shape, s):
        return jax.random.normal(k, shape, f32) * s

    a_c = jax.random.uniform(ks[19], (DEPTH, 2, D_LRU), f32, 0.9, 0.999)
    sig = a_c ** (1.0 / LRU_C)
    return {
        'x_prompt': nrm(ks[0], (BATCH, SEQ, D_MODEL), 1.0),
        'x_sample': nrm(ks[1], (DEC_BATCH, DEC_SEQ, D_MODEL), 1.0),
        'cache_k': nrm(ks[2], (DEC_BATCH, DEPTH, PAST_LEN, N_KV_HEADS, HEAD_DIM), 1.0),
        'cache_v': nrm(ks[3], (DEC_BATCH, DEPTH, PAST_LEN, N_KV_HEADS, HEAD_DIM), 1.0),
        'state_lru': nrm(ks[4], (DEC_BATCH, DEPTH, 2, D_LRU), 0.5),
        'c': nrm(ks[5], (DEC_BATCH, D_MODEL), 1.0),
        'c_ctx': nrm(ks[6], (D_MODEL,), 1.0),
        'mod_w': nrm(ks[7], (DEPTH, D_MODEL, N_MOD * D_MODEL), MOD_INIT * D_MODEL ** -0.5),
        'mod_b': nrm(ks[8], (DEPTH, N_MOD * D_MODEL), 0.02),
        'norm_mix': 1.0 + nrm(ks[9], (DEPTH, D_MODEL), 0.02),
        'norm_ffn': 1.0 + nrm(ks[10], (DEPTH, D_MODEL), 0.02),
        'w_in': nrm(ks[11], (DEPTH, D_MODEL, D_IN), D_MODEL ** -0.5),
        'attn_sink': nrm(ks[12], (DEPTH, N_HEADS), 0.5),
        'conv_w': nrm(ks[13], (DEPTH, CONV_W, D_LRU), CONV_W ** -0.5),
        'conv_b': nrm(ks[14], (DEPTH, D_LRU), 0.02),
        'lru_wa': nrm(ks[15], (DEPTH, 2, LRU_BLOCKS, LRU_BLK, LRU_BLK), LRU_BLK ** -0.5),
        'lru_ba': nrm(ks[16], (DEPTH, 2, D_LRU), 0.02),
        'lru_wx': nrm(ks[17], (DEPTH, 2, LRU_BLOCKS, LRU_BLK, LRU_BLK), LRU_BLK ** -0.5),
        'lru_bx': nrm(ks[18], (DEPTH, 2, D_LRU), 0.02),
        'lru_lambda': jnp.log(sig) - jnp.log1p(-sig),
        'pool_w': nrm(ks[20], (DEPTH, N_POOL_GROUPS, POOL_GROUP, POOL_GROUP), POOL_GROUP ** -0.5),
        'pool_scale': 1.0 + nrm(ks[21], (DEPTH, D_POOL), 0.02),
        'w_out': nrm(ks[22], (DEPTH, D_MIX, D_MODEL), D_MIX ** -0.5),
        'ffn_w1': nrm(ks[23], (DEPTH, D_MODEL, 2 * D_FF), D_MODEL ** -0.5),
        'ffn_w2': nrm(ks[24], (DEPTH, D_FF, D_MODEL), D_FF ** -0.5),
        'norm_final': 1.0 + nrm(ks[25], (D_MODEL,), 0.02),
    }


def reference(x_prompt, x_sample, cache_k, cache_v, state_lru, c, c_ctx, mod_w, mod_b, norm_mix, norm_ffn, w_in, attn_sink, conv_w, conv_b, lru_wa, lru_ba, lru_wx, lru_bx, lru_lambda, pool_w, pool_scale, w_out, ffn_w1, ffn_w2, norm_final):
    xp = x_prompt
    xs = x_sample
    ks_new, vs_new, hs_new = [], [], []
    for l in range(DEPTH):
        p = {
            'norm_mix': norm_mix[l], 'norm_ffn': norm_ffn[l], 'w_in': w_in[l],
            'sink': attn_sink[l].reshape(N_KV_HEADS, KV_GROUP),
            'conv_w': conv_w[l], 'conv_b': conv_b[l],
            'lru_wa': lru_wa[l], 'lru_ba': lru_ba[l], 'lru_wx': lru_wx[l], 'lru_bx': lru_bx[l],
            'lru_lambda': lru_lambda[l], 'pool_w': pool_w[l], 'pool_scale': pool_scale[l],
            'w_out': w_out[l], 'ffn_w1': ffn_w1[l], 'ffn_w2': ffn_w2[l],
        }
        mod_ctx = _mod_vectors(c_ctx[None, :], mod_w[l], mod_b[l])
        xp, k_l, v_l, h_l = _context_layer(xp, mod_ctx, p)
        ks_new.append(k_l)
        vs_new.append(v_l)
        hs_new.append(h_l)
        mod_lat = _mod_vectors(c, mod_w[l], mod_b[l])
        xs = _latent_layer(xs, mod_lat, cache_k[:, l], cache_v[:, l], state_lru[:, l], p)
    y_prompt = _rmsnorm(xp, norm_final)
    y_sample = _rmsnorm(xs, norm_final)
    new_cache_k = jnp.stack(ks_new, axis=1)
    new_cache_v = jnp.stack(vs_new, axis=1)
    new_state_lru = jnp.stack(hs_new, axis=1)
    return (y_prompt, y_sample, new_cache_k, new_cache_v, new_state_lru)
```

```python
import functools
import math

import jax
import jax.numpy as jnp
from jax import lax
from jax.experimental import pallas as pl
from jax.experimental.pallas import tpu as pltpu

F32 = jnp.float32
BF16 = jnp.bfloat16

D_MODEL = 2048
BATCH = 32
SEQ = 256
DEPTH = 4
DEC_BATCH = 2
DEC_SEQ = 2048
PAST_LEN = 512
GRID_W = 64
HEAD_DIM = 128
N_HEADS = 8
N_KV_HEADS = 2
KV_GROUP = N_HEADS // N_KV_HEADS
D_ATTN = N_HEADS * HEAD_DIM
D_KV = N_KV_HEADS * HEAD_DIM
WINDOW = 128
Q_BLOCK = 128
ROPE_THETA = 10000.0
D_LRU = D_MODEL // 4
LRU_BLOCKS = 4
LRU_BLK = D_LRU // LRU_BLOCKS
CONV_W = 4
LRU_C = 8.0
D_POOL = D_MODEL // 4
POOL_WINDOWS = (2, 4, 8, 16)
POOL_GROUP = D_POOL // len(POOL_WINDOWS)
D_IN = D_ATTN + 2 * D_KV + 2 * D_LRU + D_POOL
D_FF = 5632
N_MOD = 6
RMS_EPS = 1e-6
NEG_INF = -1e30

N_CTX = BATCH * SEQ
N_LAT = DEC_BATCH * DEC_SEQ
N_TOK = N_CTX + N_LAT
N_GROUPS = 1 + DEC_BATCH
COND_ROWS = 16

COL_Q = 0
COL_K = D_ATTN
COL_V = D_ATTN + D_KV
COL_LX = D_ATTN + 2 * D_KV
COL_LG = COL_LX + D_LRU
COL_PU = COL_LG + D_LRU

VMEM_LIMIT = 56 * 1024 * 1024

NORM_ROWS = 32
LANES = 128
SUBLANES = 8

CHUNK = 128
CHUNK_PITCH = 136
LRU_TILE_CHUNKS = 48
LRU_TILE = LRU_TILE_CHUNKS * CHUNK
N_CHUNKS = N_TOK // CHUNK
CTX_CHUNKS = N_CTX // CHUNK

POOL_TILE = 1024
POOL_CHUNK = 256
POOL_HALO = 8


def _dot(a, b):
    return jnp.dot(a, b, preferred_element_type=F32)


def _dot_nt(a, b):
    return lax.dot_general(a, b, (((1,), (1,)), ((), ())), preferred_element_type=F32)


def _sigmoid(x):
    return 1.0 / (1.0 + jnp.exp(-x))


def _silu(x):
    return x * _sigmoid(x)


def _gelu_tanh(x):
    c = math.sqrt(2.0 / math.pi)
    return 0.5 * x * (1.0 + jnp.tanh(c * (x + 0.044715 * (x * x * x))))


def _params(*sem):
    return pltpu.CompilerParams(dimension_semantics=sem, vmem_limit_bytes=VMEM_LIMIT)


def _group_of_tile(i, tm):
    n_ctx_tiles = N_CTX // tm
    tiles_per_lat = DEC_SEQ // tm
    return jnp.where(i < n_ctx_tiles, 0, 1 + (i - n_ctx_tiles) // tiles_per_lat)


def _mod_spec(tm, which, tn=None):
    if tn is None:
        return pl.BlockSpec((None, 1, D_MODEL), lambda i, j: (_group_of_tile(i, tm) * N_MOD + which, 0, 0))
    return pl.BlockSpec((None, 1, tn), lambda i, j: (_group_of_tile(i, tm) * N_MOD + which, 0, j))


def _norm_mod_store(h_ref, x_ref, g, scale, shift, rows):
    gain = g * 1.0
    one_plus = 1.0 + scale

    def body(r, carry):
        r0 = pl.multiple_of(r * NORM_ROWS, NORM_ROWS)
        x = x_ref[pl.ds(r0, NORM_ROWS), :]
        ms = jnp.mean(x * x, axis=-1, keepdims=True)
        y = x * lax.rsqrt(ms + RMS_EPS) * gain
        h_ref[pl.ds(r0, NORM_ROWS), :] = (y * one_plus + shift).astype(h_ref.dtype)
        return carry

    lax.fori_loop(0, rows // NORM_ROWS, body, 0)


MOD_TN = 1024


def _mod_kernel(cond_ref, w_ref, b_ref, o_ref):
    s = _silu(cond_ref[...]).astype(BF16)
    o_ref[...] = _dot(s, w_ref[...].astype(BF16)) + b_ref[...]


def _mod_vectors(cond, mod_w, mod_b):
    n = N_MOD * D_MODEL
    return pl.pallas_call(
        _mod_kernel,
        grid=(DEPTH, n // MOD_TN),
        in_specs=[
            pl.BlockSpec((COND_ROWS, D_MODEL), lambda l, j: (0, 0)),
            pl.BlockSpec((None, D_MODEL, MOD_TN), lambda l, j: (l, 0, j)),
            pl.BlockSpec((None, 1, MOD_TN), lambda l, j: (l, 0, j)),
        ],
        out_specs=pl.BlockSpec((None, COND_ROWS, MOD_TN), lambda l, j: (l, 0, j)),
        out_shape=jax.ShapeDtypeStruct((DEPTH, COND_ROWS, n), F32),
        compiler_params=_params("arbitrary", "arbitrary"),
        name="mod_vectors",
    )(cond, mod_w, mod_b.reshape(DEPTH, 1, n))


INPROJ_TM = 1024
INPROJ_TN = 1024


def _inproj_kernel(x_ref, g_ref, shift_ref, scale_ref, w_ref, o_ref, h_ref):
    @pl.when(pl.program_id(1) == 0)
    def _():
        _norm_mod_store(h_ref, x_ref, g_ref[...], scale_ref[...], shift_ref[...], INPROJ_TM)

    o_ref[...] = _dot(h_ref[...], w_ref[...])


def _inproj(x, norm_g, mod, w_in_bf16, layer):
    tm, tn = INPROJ_TM, INPROJ_TN
    return pl.pallas_call(
        _inproj_kernel,
        grid=(N_TOK // tm, D_IN // tn),
        in_specs=[
            pl.BlockSpec((tm, D_MODEL), lambda i, j: (i, 0)),
            pl.BlockSpec((1, D_MODEL), lambda i, j: (0, 0)),
            _mod_spec(tm, 0),
            _mod_spec(tm, 1),
            pl.BlockSpec((None, D_MODEL, tn), lambda i, j: (layer, 0, j)),
        ],
        out_specs=pl.BlockSpec((tm, tn), lambda i, j: (i, j)),
        out_shape=jax.ShapeDtypeStruct((N_TOK, D_IN), F32),
        scratch_shapes=[pltpu.VMEM((tm, D_MODEL), BF16)],
        compiler_params=_params("arbitrary", "arbitrary"),
        name="inproj",
    )(x, norm_g, mod, mod, w_in_bf16)


def _sink_column(sink_ref, kv_head, rows_per_head):
    n = KV_GROUP * rows_per_head
    row = lax.broadcasted_iota(jnp.int32, (n, 1), 0)
    col = jnp.full((n, 1), sink_ref[kv_head * KV_GROUP + KV_GROUP - 1], F32)
    for hh in range(KV_GROUP - 2, -1, -1):
        col = jnp.where(row < (hh + 1) * rows_per_head, sink_ref[kv_head * KV_GROUP + hh], col)
    return col


def _softmax_pv(scores, values, sink_col):
    m = sink_col
    for s in scores:
        m = jnp.maximum(m, jnp.max(s, axis=-1, keepdims=True))
    denom = jnp.exp(sink_col - m)
    out = None
    for s, v in zip(scores, values):
        p = jnp.exp(s - m)
        denom = denom + jnp.sum(p, axis=-1, keepdims=True)
        pv = _dot(p.astype(BF16), v)
        out = pv if out is None else out + pv
    return out / denom


def _stack_heads(q, kv_head):
    return jnp.concatenate(
        [q[:, (kv_head * KV_GROUP + hh) * HEAD_DIM:(kv_head * KV_GROUP + hh + 1) * HEAD_DIM]
         for hh in range(KV_GROUP)], axis=0)


def _ctx_attn_kernel(sink_ref, q_ref, kv_ref, o_ref):
    scale = HEAD_DIM ** -0.5
    q = q_ref[...]
    kv = kv_ref[...]
    for g in range(N_KV_HEADS):
        k = kv[:, g * HEAD_DIM:(g + 1) * HEAD_DIM].astype(BF16)
        v = kv[:, D_KV + g * HEAD_DIM:D_KV + (g + 1) * HEAD_DIM].astype(BF16)
        qs = _stack_heads(q, g).astype(BF16)
        s = _dot_nt(qs, k) * scale
        o = _softmax_pv([s], [v], _sink_column(sink_ref, g, SEQ))
        for hh in range(KV_GROUP):
            h = g * KV_GROUP + hh
            o_ref[:, h * HEAD_DIM:(h + 1) * HEAD_DIM] = o[hh * SEQ:(hh + 1) * SEQ].astype(o_ref.dtype)


def _ctx_attention(p, sink):
    kv_blk = 2 * D_KV
    return pl.pallas_call(
        _ctx_attn_kernel,
        grid_spec=pltpu.PrefetchScalarGridSpec(
            num_scalar_prefetch=1,
            grid=(BATCH,),
            in_specs=[
                pl.BlockSpec((SEQ, D_ATTN), lambda b, s: (b, 0)),
                pl.BlockSpec((SEQ, kv_blk), lambda b, s: (b, COL_K // kv_blk)),
            ],
            out_specs=pl.BlockSpec((SEQ, D_ATTN), lambda b, s: (b, 0)),
        ),
        out_shape=jax.ShapeDtypeStruct((N_TOK, D_ATTN), BF16),
        compiler_params=_params("arbitrary"),
        name="ctx_attention",
    )(sink, p, p)


def _rope(x, cos, sin_lo, sin_hi):
    return x * cos + pltpu.roll(x, 96, 1) * sin_lo + pltpu.roll(x, 32, 1) * sin_hi


def _lat_attn_kernel(sink_ref, q_ref, kv_ref, ck_ref, cv_ref, cos_ref, slo_ref, shi_ref, attn_hbm_ref,
                     o_ref, k_s, v_s, ck_s, cv_s):
    del attn_hbm_ref
    i = pl.program_id(1)
    scale = HEAD_DIM ** -0.5
    band = Q_BLOCK + 2 * WINDOW

    @pl.when(i == 0)
    def _():
        for g in range(N_KV_HEADS):
            sl = slice(g * HEAD_DIM, (g + 1) * HEAD_DIM)
            k = kv_ref[:, g * HEAD_DIM:(g + 1) * HEAD_DIM]
            k_s[:, sl] = _rope(k, cos_ref[...], slo_ref[...], shi_ref[...]).astype(BF16)
            v_s[:, sl] = kv_ref[:, D_KV + g * HEAD_DIM:D_KV + (g + 1) * HEAD_DIM].astype(BF16)
        ck_s[...] = ck_ref[...].astype(BF16)
        cv_s[...] = cv_ref[...].astype(BF16)

    q0 = pl.multiple_of(i * Q_BLOCK, Q_BLOCK)
    k0 = pl.multiple_of(jnp.clip(q0 - WINDOW, 0, DEC_SEQ - band), Q_BLOCK)
    cos = cos_ref[pl.ds(q0, Q_BLOCK), :]
    slo = slo_ref[pl.ds(q0, Q_BLOCK), :]
    shi = shi_ref[pl.ds(q0, Q_BLOCK), :]
    cos4 = jnp.concatenate([cos] * KV_GROUP, axis=0)
    slo4 = jnp.concatenate([slo] * KV_GROUP, axis=0)
    shi4 = jnp.concatenate([shi] * KV_GROUP, axis=0)

    n = KV_GROUP * Q_BLOCK
    qpos = q0 + (lax.broadcasted_iota(jnp.int32, (n, band), 0) & (Q_BLOCK - 1))
    kpos = k0 + lax.broadcasted_iota(jnp.int32, (n, band), 1)
    valid = jnp.abs(qpos - kpos) <= WINDOW

    q = q_ref[...]
    for g in range(N_KV_HEADS):
        sl = slice(g * HEAD_DIM, (g + 1) * HEAD_DIM)
        qs = _rope(_stack_heads(q, g), cos4, slo4, shi4).astype(BF16)
        kb = k_s[pl.ds(k0, band), sl]
        vb = v_s[pl.ds(k0, band), sl]
        s_band = jnp.where(valid, _dot_nt(qs, kb) * scale, NEG_INF)
        s_ctx = _dot_nt(qs, ck_s[:, sl]) * scale
        o = _softmax_pv([s_band, s_ctx], [vb, cv_s[:, sl]], _sink_column(sink_ref, g, Q_BLOCK))
        for hh in range(KV_GROUP):
            h = g * KV_GROUP + hh
            o_ref[:, h * HEAD_DIM:(h + 1) * HEAD_DIM] = (
                o[hh * Q_BLOCK:(hh + 1) * Q_BLOCK].astype(o_ref.dtype))


def _lat_attention(p, sink, cache_k, cache_v, rope_tabs, attn, layer):
    kv_blk = 2 * D_KV
    nq = DEC_SEQ // Q_BLOCK
    ctx_rows = N_CTX // Q_BLOCK
    seq_blk = N_CTX // DEC_SEQ
    tab_spec = pl.BlockSpec((DEC_SEQ, HEAD_DIM), lambda b, i, s: (0, 0))
    cache_spec = pl.BlockSpec((None, None, PAST_LEN, D_KV), lambda b, i, s: (b, layer, 0, 0))
    return pl.pallas_call(
        _lat_attn_kernel,
        grid_spec=pltpu.PrefetchScalarGridSpec(
            num_scalar_prefetch=1,
            grid=(DEC_BATCH, nq),
            in_specs=[
                pl.BlockSpec((Q_BLOCK, D_ATTN), lambda b, i, s: (ctx_rows + b * nq + i, 0)),
                pl.BlockSpec((DEC_SEQ, kv_blk), lambda b, i, s: (seq_blk + b, COL_K // kv_blk)),
                cache_spec, cache_spec, tab_spec, tab_spec, tab_spec,
                pl.BlockSpec(memory_space=pl.ANY),
            ],
            out_specs=pl.BlockSpec((Q_BLOCK, D_ATTN), lambda b, i, s: (ctx_rows + b * nq + i, 0)),
            scratch_shapes=[
                pltpu.VMEM((DEC_SEQ, D_KV), BF16), pltpu.VMEM((DEC_SEQ, D_KV), BF16),
                pltpu.VMEM((PAST_LEN, D_KV), BF16), pltpu.VMEM((PAST_LEN, D_KV), BF16),
            ],
        ),
        out_shape=jax.ShapeDtypeStruct((N_TOK, D_ATTN), BF16),
        input_output_aliases={8: 0},
        compiler_params=_params("arbitrary", "arbitrary"),
        name="lat_attention",
    )(sink, p, p, cache_k, cache_v, *rope_tabs, attn)


def _rope_tables():
    t = jnp.arange(DEC_SEQ)
    pos = jnp.stack([t // GRID_W, t % GRID_W], axis=-1).astype(F32)
    rd = HEAD_DIM // 4
    inv = ROPE_THETA ** (-jnp.arange(rd, dtype=F32) / rd)
    ang = jnp.broadcast_to(pos[:, :, None, None] * inv, (DEC_SEQ, 2, 2, rd)).reshape(DEC_SEQ, HEAD_DIM)
    cos, sin = jnp.cos(ang), jnp.sin(ang)
    first = (jnp.arange(HEAD_DIM) % (2 * rd)) < rd
    return cos, jnp.where(first, -sin, 0.0), jnp.where(first, 0.0, sin)


def _seq_position(chunk):
    ctx_per_seq = SEQ // CHUNK
    lat_per_seq = DEC_SEQ // CHUNK
    is_ctx = chunk < CTX_CHUNKS
    idx = jnp.where(is_ctx, chunk % ctx_per_seq, (chunk - CTX_CHUNKS) % lat_per_seq)
    per = jnp.where(is_ctx, ctx_per_seq, lat_per_seq)
    return idx == 0, idx == per - 1


def _lru_kernel(lx_ref, lg_ref, cw_ref, cb_ref, w4_ref, b4_ref, lam_ref, keep_ref, h0_ref,
                o_ref, hend_ref, a_s, b_s, tot_s, loc_s, hin_s):
    tile = pl.program_id(1)
    nch = LRU_TILE_CHUNKS
    row = lax.broadcasted_iota(jnp.int32, (CHUNK, LANES), 0)
    lam = lam_ref[...]
    log_sig = jnp.minimum(lam, 0.0) - jnp.log(1.0 + jnp.exp(-jnp.abs(lam)))
    cw = cw_ref[...]
    cb = cb_ref[...]
    w4 = w4_ref[...]
    b4 = b4_ref[...]

    def gates(c, carry):
        r0 = pl.multiple_of(c * CHUNK, CHUNK)
        first, last = _seq_position(tile * nch + c)
        lo = pl.multiple_of(jnp.maximum(r0 - SUBLANES, 0), SUBLANES)
        hi = pl.multiple_of(jnp.minimum(r0 + CHUNK, LRU_TILE - SUBLANES), SUBLANES)
        ext = jnp.concatenate(
            [lx_ref[pl.ds(lo, SUBLANES), :], lx_ref[pl.ds(r0, CHUNK), :], lx_ref[pl.ds(hi, SUBLANES), :]],
            axis=0)
        xm2 = jnp.where(jnp.logical_and(first, row < 2), 0.0, ext[6:6 + CHUNK])
        xm1 = jnp.where(jnp.logical_and(first, row < 1), 0.0, ext[7:7 + CHUNK])
        x0 = ext[8:8 + CHUNK]
        xp1 = jnp.where(jnp.logical_and(last, row >= CHUNK - 1), 0.0, ext[9:9 + CHUNK])
        u = xm2 * cw[0:1] + xm1 * cw[1:2] + x0 * cw[2:3] + xp1 * cw[3:4] + cb
        g = _dot(u.astype(BF16), w4) + b4
        s0 = pl.multiple_of(c * CHUNK_PITCH, SUBLANES)
        for d in range(2):
            r = _sigmoid(g[:, (2 * d) * LANES:(2 * d + 1) * LANES])
            ig = _sigmoid(g[:, (2 * d + 1) * LANES:(2 * d + 2) * LANES])
            log_a = LRU_C * r * log_sig[d:d + 1]
            a_s[d, pl.ds(s0, CHUNK), :] = jnp.exp(log_a)
            b_s[d, pl.ds(s0, CHUNK), :] = jnp.sqrt(1.0 - jnp.exp(2.0 * log_a)) * ig * u
        return carry

    lax.fori_loop(0, nch, gates, 0)

    for d in range(2):
        def step(k, carry):
            acc_a, h = carry
            t = k if d == 0 else CHUNK - 1 - k
            idx = pl.ds(t, nch, stride=CHUNK_PITCH)
            a_t = a_s[d, idx, :]
            h = a_t * h + b_s[d, idx, :]
            acc_a = acc_a * a_t
            a_s[d, idx, :] = acc_a
            b_s[d, idx, :] = h
            return acc_a, h

        tot, loc = lax.fori_loop(
            0, CHUNK, step, (jnp.ones((nch, LANES), F32), jnp.zeros((nch, LANES), F32)))
        tot_s[d] = tot
        loc_s[d] = loc

    for d in range(2):
        prev = jnp.zeros((1, LANES), F32)
        order = range(nch) if d == 0 else range(nch - 1, -1, -1)
        for c in order:
            hin = keep_ref[d, c:c + 1, :] * prev + h0_ref[d, c:c + 1, :]
            hin_s[d, c:c + 1, :] = hin
            prev = tot_s[d, c:c + 1, :] * hin + loc_s[d, c:c + 1, :]
            hend_ref[d, c:c + 1, :] = prev

    def emit(c, carry):
        r0 = pl.multiple_of(c * CHUNK, CHUNK)
        s0 = pl.multiple_of(c * CHUNK_PITCH, SUBLANES)
        y = None
        for d in range(2):
            h = b_s[d, pl.ds(s0, CHUNK), :] + a_s[d, pl.ds(s0, CHUNK), :] * hin_s[d, pl.ds(c, 1), :]
            y = h if y is None else y + h
        o_ref[pl.ds(r0, CHUNK), :] = (y * _gelu_tanh(lg_ref[pl.ds(r0, CHUNK), :])).astype(o_ref.dtype)
        return carry

    lax.fori_loop(0, nch, emit, 0)


def _lru(p, conv_w, conv_b, w4, b4, lam, keep, h0):
    nt = N_TOK // LRU_TILE
    nch = LRU_TILE_CHUNKS
    lx_blk = COL_LX // LRU_BLK
    lg_blk = COL_LG // LRU_BLK
    return pl.pallas_call(
        _lru_kernel,
        grid=(LRU_BLOCKS, nt),
        in_specs=[
            pl.BlockSpec((LRU_TILE, LRU_BLK), lambda n, r: (r, lx_blk + n)),
            pl.BlockSpec((LRU_TILE, LRU_BLK), lambda n, r: (r, lg_blk + n)),
            pl.BlockSpec((CONV_W, LRU_BLK), lambda n, r: (0, n)),
            pl.BlockSpec((1, LRU_BLK), lambda n, r: (0, n)),
            pl.BlockSpec((None, LRU_BLK, 4 * LRU_BLK), lambda n, r: (n, 0, 0)),
            pl.BlockSpec((None, 1, 4 * LRU_BLK), lambda n, r: (n, 0, 0)),
            pl.BlockSpec((2, LRU_BLK), lambda n, r: (0, n)),
            pl.BlockSpec((2, nch, LRU_BLK), lambda n, r: (0, r, 0)),
            pl.BlockSpec((2, nch, LRU_BLK), lambda n, r: (0, r, n)),
        ],
        out_specs=[
            pl.BlockSpec((LRU_TILE, LRU_BLK), lambda n, r: (r, n)),
            pl.BlockSpec((2, nch, LRU_BLK), lambda n, r: (0, r, n)),
        ],
        out_shape=[
            jax.ShapeDtypeStruct((N_TOK, D_LRU), BF16),
            jax.ShapeDtypeStruct((2, N_CHUNKS, D_LRU), F32),
        ],
        scratch_shapes=[
            pltpu.VMEM((2, nch * CHUNK_PITCH, LANES), F32),
            pltpu.VMEM((2, nch * CHUNK_PITCH, LANES), F32),
            pltpu.VMEM((2, nch, LANES), F32),
            pltpu.VMEM((2, nch, LANES), F32),
            pltpu.VMEM((2, nch, LANES), F32),
        ],
        compiler_params=_params("arbitrary", "arbitrary"),
        name="rglru",
    )(p, p, conv_w, conv_b, w4, b4, lam, keep, h0)


def _lru_chain_inputs(state_l):
    c = jnp.arange(N_CHUNKS)
    ctx_per_seq = SEQ // CHUNK
    lat_per_seq = DEC_SEQ // CHUNK
    is_ctx = c < CTX_CHUNKS
    idx = jnp.where(is_ctx, c % ctx_per_seq, (c - CTX_CHUNKS) % lat_per_seq)
    per = jnp.where(is_ctx, ctx_per_seq, lat_per_seq)
    start = jnp.stack([idx == 0, idx == per - 1])
    keep = jnp.broadcast_to(jnp.where(start, 0.0, 1.0)[:, :, None], (2, N_CHUNKS, LANES)).astype(F32)
    lat_b = jnp.clip((c - CTX_CHUNKS) // lat_per_seq, 0, DEC_BATCH - 1)
    seed = jnp.transpose(state_l, (1, 0, 2))[:, lat_b, :]
    h0 = jnp.where((start & ~is_ctx[None, :])[:, :, None], seed, 0.0).astype(F32)
    return keep, h0


def _pool_kernel(x_ref, w_ref, sc_ref, o_ref):
    tile = pl.program_id(0)
    ext_rows = POOL_CHUNK + 2 * POOL_HALO
    is_ctx = tile * POOL_TILE < N_CTX
    erow = lax.broadcasted_iota(jnp.int32, (ext_rows, LANES), 0)
    row = lax.broadcasted_iota(jnp.int32, (POOL_CHUNK, LANES), 0)

    def shift_up(x, k):
        return x if k == 0 else pltpu.roll(x, ext_rows - k, 0)

    for g, win in enumerate(POOL_WINDOWS):
        half = win // 2
        cols = slice(g * POOL_GROUP, (g + 1) * POOL_GROUP)
        w = w_ref[g]
        sc = sc_ref[:, cols]

        def body(c, carry):
            r0 = pl.multiple_of(c * POOL_CHUNK, POOL_CHUNK)
            seq_len = jnp.where(is_ctx, SEQ, DEC_SEQ)
            t0 = jnp.where(is_ctx, 0, (tile * POOL_TILE + r0) & (DEC_SEQ - 1))
            first = t0 == 0
            last = t0 + POOL_CHUNK == seq_len
            lo = pl.multiple_of(jnp.maximum(r0 - POOL_HALO, 0), POOL_HALO)
            hi = pl.multiple_of(jnp.minimum(r0 + POOL_CHUNK, POOL_TILE - POOL_HALO), POOL_HALO)
            cur = x_ref[pl.ds(r0, POOL_CHUNK), cols]
            ext = jnp.concatenate(
                [x_ref[pl.ds(lo, POOL_HALO), cols], cur, x_ref[pl.ds(hi, POOL_HALO), cols]], axis=0)
            outside = jnp.logical_or(jnp.logical_and(first, erow < POOL_HALO),
                                     jnp.logical_and(last, erow >= POOL_HALO + POOL_CHUNK))
            acc = jnp.where(outside, 0.0, ext)
            span = 1
            while span < win:
                acc = acc + shift_up(acc, span)
                span *= 2
            total = shift_up(acc, POOL_HALO - half)[:POOL_CHUNK]
            t = t0 + row
            cnt = jnp.minimum(t + half, seq_len) - jnp.maximum(t - half, 0)
            pooled = total / cnt.astype(F32) - cur
            y = _dot(pooled.astype(BF16), w) * sc
            o_ref[pl.ds(r0, POOL_CHUNK), cols] = y.astype(o_ref.dtype)
            return carry

        lax.fori_loop(0, POOL_TILE // POOL_CHUNK, body, 0)


def _pool(p, pool_w_bf16, pool_scale):
    blk = COL_PU // D_POOL
    return pl.pallas_call(
        _pool_kernel,
        grid=(N_TOK // POOL_TILE,),
        in_specs=[
            pl.BlockSpec((POOL_TILE, D_POOL), lambda i: (i, blk)),
            pl.BlockSpec((len(POOL_WINDOWS), POOL_GROUP, POOL_GROUP), lambda i: (0, 0, 0)),
            pl.BlockSpec((1, D_POOL), lambda i: (0, 0)),
        ],
        out_specs=pl.BlockSpec((POOL_TILE, D_POOL), lambda i: (i, 0)),
        out_shape=jax.ShapeDtypeStruct((N_TOK, D_POOL), BF16),
        compiler_params=_params("arbitrary"),
        name="pool_mixer",
    )(p, pool_w_bf16, pool_scale)


OUTPROJ_TM = 1024
OUTPROJ_TN = 1024


def _outproj_kernel(attn_ref, lru_ref, pool_ref, x_ref, gate_ref, wa_ref, wl_ref, wp_ref, o_ref):
    mix = _dot(attn_ref[...], wa_ref[...]) + _dot(lru_ref[...], wl_ref[...]) + _dot(pool_ref[...], wp_ref[...])
    o_ref[...] = x_ref[...] + gate_ref[...] * mix


def _outproj(attn, lru, pool, x, mod, w_out_bf16, layer):
    tm, tn = OUTPROJ_TM, OUTPROJ_TN
    return pl.pallas_call(
        _outproj_kernel,
        grid=(N_TOK // tm, D_MODEL // tn),
        in_specs=[
            pl.BlockSpec((tm, D_ATTN), lambda i, j: (i, 0)),
            pl.BlockSpec((tm, D_LRU), lambda i, j: (i, 0)),
            pl.BlockSpec((tm, D_POOL), lambda i, j: (i, 0)),
            pl.BlockSpec((tm, tn), lambda i, j: (i, j)),
            _mod_spec(tm, 2, tn),
            pl.BlockSpec((None, D_ATTN, tn), lambda i, j: (layer, 0, j)),
            pl.BlockSpec((None, D_LRU, tn), lambda i, j: (layer, D_ATTN // D_LRU, j)),
            pl.BlockSpec((None, D_POOL, tn), lambda i, j: (layer, (D_ATTN + D_LRU) // D_POOL, j)),
        ],
        out_specs=pl.BlockSpec((tm, tn), lambda i, j: (i, j)),
        out_shape=jax.ShapeDtypeStruct((N_TOK, D_MODEL), F32),
        compiler_params=_params("arbitrary", "arbitrary"),
        name="outproj",
    )(attn, lru, pool, x, mod, w_out_bf16, w_out_bf16, w_out_bf16)


FFN_TM = 512
FFN_TF = 512


def _ffn_kernel(x_ref, g_ref, shift_ref, scale_ref, gate_ref, wg_ref, wu_ref, w2_ref, o_ref, h_ref):
    j = pl.program_id(1)

    @pl.when(j == 0)
    def _():
        _norm_mod_store(h_ref, x_ref, g_ref[...], scale_ref[...], shift_ref[...], FFN_TM)

    h = h_ref[...]
    act = (_silu(_dot(h, wg_ref[...])) * _dot(h, wu_ref[...])).astype(BF16)
    part = _dot(act, w2_ref[...])

    @pl.when(j == 0)
    def _():
        o_ref[...] = part

    @pl.when(j > 0)
    def _():
        o_ref[...] += part

    @pl.when(j == pl.num_programs(1) - 1)
    def _():
        o_ref[...] = x_ref[...] + gate_ref[...] * o_ref[...]


def _ffn(x, norm_g, mod, w1_bf16, w2_bf16, layer):
    tm, tf = FFN_TM, FFN_TF
    nf = D_FF // tf
    return pl.pallas_call(
        _ffn_kernel,
        grid=(N_TOK // tm, nf),
        in_specs=[
            pl.BlockSpec((tm, D_MODEL), lambda i, j: (i, 0)),
            pl.BlockSpec((1, D_MODEL), lambda i, j: (0, 0)),
            _mod_spec(tm, 3),
            _mod_spec(tm, 4),
            _mod_spec(tm, 5),
            pl.BlockSpec((None, D_MODEL, tf), lambda i, j: (layer, 0, j)),
            pl.BlockSpec((None, D_MODEL, tf), lambda i, j: (layer, 0, nf + j)),
            pl.BlockSpec((None, tf, D_MODEL), lambda i, j: (layer, j, 0)),
        ],
        out_specs=pl.BlockSpec((tm, D_MODEL), lambda i, j: (i, 0)),
        out_shape=jax.ShapeDtypeStruct((N_TOK, D_MODEL), F32),
        scratch_shapes=[pltpu.VMEM((tm, D_MODEL), BF16)],
        compiler_params=_params("arbitrary", "arbitrary"),
        name="ffn",
    )(x, norm_g, mod, mod, mod, w1_bf16, w1_bf16, w2_bf16)


FINAL_TM = 512


def _final_norm_kernel(x_ref, g_ref, o_ref):
    gain = g_ref[...]

    def body(r, carry):
        r0 = pl.multiple_of(r * NORM_ROWS, NORM_ROWS)
        x = x_ref[pl.ds(r0, NORM_ROWS), :]
        ms = jnp.mean(x * x, axis=-1, keepdims=True)
        o_ref[pl.ds(r0, NORM_ROWS), :] = x * lax.rsqrt(ms + RMS_EPS) * gain
        return carry

    lax.fori_loop(0, FINAL_TM // NORM_ROWS, body, 0)


def _final_norm(x, g):
    return pl.pallas_call(
        _final_norm_kernel,
        grid=(N_TOK // FINAL_TM,),
        in_specs=[
            pl.BlockSpec((FINAL_TM, D_MODEL), lambda i: (i, 0)),
            pl.BlockSpec((1, D_MODEL), lambda i: (0, 0)),
        ],
        out_specs=pl.BlockSpec((FINAL_TM, D_MODEL), lambda i: (i, 0)),
        out_shape=jax.ShapeDtypeStruct((N_TOK, D_MODEL), F32),
        compiler_params=_params("arbitrary"),
        name="final_norm",
    )(x, g)


def kernel(x_prompt, x_sample, cache_k, cache_v, state_lru, c, c_ctx, mod_w, mod_b, norm_mix, norm_ffn,
           w_in, attn_sink, conv_w, conv_b, lru_wa, lru_ba, lru_wx, lru_bx, lru_lambda, pool_w,
           pool_scale, w_out, ffn_w1, ffn_w2, norm_final):
    x = jnp.concatenate([x_prompt.reshape(N_CTX, D_MODEL), x_sample.reshape(N_LAT, D_MODEL)], axis=0)

    cond = jnp.zeros((COND_ROWS, D_MODEL), F32).at[0].set(c_ctx).at[1:1 + DEC_BATCH].set(c)
    mod_all = _mod_vectors(cond, mod_w, mod_b)
    mod_all = mod_all[:, :N_GROUPS].reshape(DEPTH, N_GROUPS * N_MOD, 1, D_MODEL)

    w_in_b = w_in.astype(BF16)
    w_out_b = w_out.astype(BF16)
    w1_b = ffn_w1.astype(BF16)
    w2_b = ffn_w2.astype(BF16)
    pool_w_b = pool_w.astype(BF16)
    w4 = jnp.concatenate([lru_wa[:, 0], lru_wx[:, 0], lru_wa[:, 1], lru_wx[:, 1]], axis=-1).astype(BF16)
    ba = lru_ba.reshape(DEPTH, 2, LRU_BLOCKS, LRU_BLK)
    bx = lru_bx.reshape(DEPTH, 2, LRU_BLOCKS, LRU_BLK)
    b4 = jnp.stack([ba[:, 0], bx[:, 0], ba[:, 1], bx[:, 1]], axis=2).reshape(DEPTH, LRU_BLOCKS, 1, 4 * LRU_BLK)
    cache_k2 = cache_k.reshape(DEC_BATCH, DEPTH, PAST_LEN, D_KV)
    cache_v2 = cache_v.reshape(DEC_BATCH, DEPTH, PAST_LEN, D_KV)
    rope_tabs = _rope_tables()

    ks_new, vs_new, hs_new = [], [], []
    for l in range(DEPTH):
        mod = mod_all[l]
        p = _inproj(x, norm_mix[l][None, :], mod, w_in_b, l)
        ks_new.append(p[:N_CTX, COL_K:COL_K + D_KV].reshape(BATCH, SEQ, N_KV_HEADS, HEAD_DIM))
        vs_new.append(p[:N_CTX, COL_V:COL_V + D_KV].reshape(BATCH, SEQ, N_KV_HEADS, HEAD_DIM))

        attn = _ctx_attention(p, attn_sink[l])
        attn = _lat_attention(p, attn_sink[l], cache_k2, cache_v2, rope_tabs, attn, l)
        keep, h0 = _lru_chain_inputs(state_lru[:, l])
        lru, hend = _lru(p, conv_w[l], conv_b[l][None, :], w4[l], b4[l], lru_lambda[l], keep, h0)
        pool = _pool(p, pool_w_b[l], pool_scale[l][None, :])

        ctx_per_seq = SEQ // CHUNK
        fwd_fin = hend[0, ctx_per_seq - 1:CTX_CHUNKS:ctx_per_seq]
        bwd_fin = hend[1, 0:CTX_CHUNKS:ctx_per_seq]
        hs_new.append(jnp.stack([fwd_fin, bwd_fin], axis=1))

        x = _outproj(attn, lru, pool, x, mod, w_out_b, l)
        x = _ffn(x, norm_ffn[l][None, :], mod, w1_b, w2_b, l)

    y = _final_norm(x, norm_final[None, :])
    y_prompt = y[:N_CTX].reshape(BATCH, SEQ, D_MODEL)
    y_sample = y[N_CTX:].reshape(DEC_BATCH, DEC_SEQ, D_MODEL)
    return (y_prompt, y_sample, jnp.stack(ks_new, axis=1), jnp.stack(vs_new, axis=1),
            jnp.stack(hs_new, axis=1))
```

```python
import functools
import math

import jax
import jax.numpy as jnp
from jax import lax
from jax.experimental import pallas as pl
from jax.experimental.pallas import tpu as pltpu

F32 = jnp.float32
BF16 = jnp.bfloat16

D_MODEL = 2048
BATCH = 32
SEQ = 256
DEPTH = 4
DEC_BATCH = 2
DEC_SEQ = 2048
PAST_LEN = 512
GRID_W = 64
HEAD_DIM = 128
N_HEADS = 8
N_KV_HEADS = 2
KV_GROUP = N_HEADS // N_KV_HEADS
D_ATTN = N_HEADS * HEAD_DIM
D_KV = N_KV_HEADS * HEAD_DIM
WINDOW = 128
Q_BLOCK = 128
ROPE_THETA = 10000.0
D_LRU = D_MODEL // 4
LRU_BLOCKS = 4
LRU_BLK = D_LRU // LRU_BLOCKS
CONV_W = 4
LRU_C = 8.0
D_POOL = D_MODEL // 4
POOL_WINDOWS = (2, 4, 8, 16)
POOL_GROUP = D_POOL // len(POOL_WINDOWS)
D_IN = D_ATTN + 2 * D_KV + 2 * D_LRU + D_POOL
D_FF = 5632
N_MOD = 6
RMS_EPS = 1e-6
NEG_INF = -1e30

N_CTX = BATCH * SEQ
N_LAT = DEC_BATCH * DEC_SEQ
N_TOK = N_CTX + N_LAT
N_GROUPS = 1 + DEC_BATCH
COND_ROWS = 16

COL_Q = 0
COL_K = D_ATTN
COL_V = D_ATTN + D_KV
COL_LX = D_ATTN + 2 * D_KV
COL_LG = COL_LX + D_LRU
COL_PU = COL_LG + D_LRU

VMEM_LIMIT = 56 * 1024 * 1024

NORM_ROWS = 32
NORM_UNROLL = 4
LANES = 128
SUBLANES = 8

CHUNK = 128
CHUNK_PITCH = 136
LRU_TILE_CHUNKS = 48
LRU_TILE = LRU_TILE_CHUNKS * CHUNK
N_CHUNKS = N_TOK // CHUNK
CTX_CHUNKS = N_CTX // CHUNK

POOL_TILE = DEC_SEQ
POOL_CHUNK = 256
POOL_HALO = 8


def _dot(a, b):
    return jnp.dot(a, b, preferred_element_type=F32)


def _dot_nt(a, b):
    return lax.dot_general(a, b, (((1,), (1,)), ((), ())), preferred_element_type=F32)


def _sigmoid(x):
    return 1.0 / (1.0 + jnp.exp(-x))


def _sigmoid_tanh(x):
    return 0.5 * jnp.tanh(0.5 * x) + 0.5


def _silu(x):
    return x * _sigmoid(x)


def _gelu_tanh(x):
    c = math.sqrt(2.0 / math.pi)
    return 0.5 * x * (1.0 + jnp.tanh(c * (x + 0.044715 * (x * x * x))))


def _params(*sem):
    return pltpu.CompilerParams(dimension_semantics=sem, vmem_limit_bytes=VMEM_LIMIT)


def _group_of_tile(i, tm):
    n_ctx_tiles = N_CTX // tm
    tiles_per_lat = DEC_SEQ // tm
    return jnp.where(i < n_ctx_tiles, 0, 1 + (i - n_ctx_tiles) // tiles_per_lat)


def _mod_spec(tm, which, tn=None):
    if tn is None:
        return pl.BlockSpec((None, 1, D_MODEL), lambda i, j: (_group_of_tile(i, tm) * N_MOD + which, 0, 0))
    return pl.BlockSpec((None, 1, tn), lambda i, j: (_group_of_tile(i, tm) * N_MOD + which, 0, j))


def _norm_mod_store(h_ref, x_ref, g, scale, shift, rows, copy_ref=None):
    gain = g * (1.0 + scale)

    def body(r, carry):
        r0 = pl.multiple_of(r * NORM_ROWS, NORM_ROWS)
        x = x_ref[pl.ds(r0, NORM_ROWS), :]
        ms = jnp.mean(x * x, axis=-1, keepdims=True)
        h_ref[pl.ds(r0, NORM_ROWS), :] = (x * lax.rsqrt(ms + RMS_EPS) * gain + shift).astype(h_ref.dtype)
        if copy_ref is not None:
            copy_ref[pl.ds(r0, NORM_ROWS), :] = x
        return carry

    lax.fori_loop(0, rows // NORM_ROWS, body, 0, unroll=NORM_UNROLL)


MOD_TN = 1024


def _mod_kernel(cond_ref, w_ref, b_ref, o_ref):
    s = _silu(cond_ref[...]).astype(BF16)
    o_ref[...] = _dot(s, w_ref[...].astype(BF16)) + b_ref[...]


def _mod_vectors(cond, mod_w, mod_b):
    n = N_MOD * D_MODEL
    return pl.pallas_call(
        _mod_kernel,
        grid=(DEPTH, n // MOD_TN),
        in_specs=[
            pl.BlockSpec((COND_ROWS, D_MODEL), lambda l, j: (0, 0)),
            pl.BlockSpec((None, D_MODEL, MOD_TN), lambda l, j: (l, 0, j)),
            pl.BlockSpec((None, 1, MOD_TN), lambda l, j: (l, 0, j)),
        ],
        out_specs=pl.BlockSpec((None, COND_ROWS, MOD_TN), lambda l, j: (l, 0, j)),
        out_shape=jax.ShapeDtypeStruct((DEPTH, COND_ROWS, n), F32),
        compiler_params=_params("arbitrary", "arbitrary"),
        name="mod_vectors",
    )(cond, mod_w, mod_b.reshape(DEPTH, 1, n))


INPROJ_TM = 1024
INPROJ_TN = 1024


def _inproj_kernel(x_ref, g_ref, shift_ref, scale_ref, w_ref, kc_in_ref, vc_in_ref,
                   o_ref, kc_ref, vc_ref, h_ref):
    del kc_in_ref, vc_in_ref
    i, j = pl.program_id(0), pl.program_id(1)

    @pl.when(j == 0)
    def _():
        _norm_mod_store(h_ref, x_ref, g_ref[...], scale_ref[...], shift_ref[...], INPROJ_TM)

    o_ref[...] = _dot(h_ref[...], w_ref[...])

    @pl.when(jnp.logical_and(j == COL_K // INPROJ_TN, i < N_CTX // INPROJ_TM))
    def _():
        k0 = COL_K % INPROJ_TN
        v0 = COL_V % INPROJ_TN
        kc_ref[...] = o_ref[:, k0:k0 + D_KV].reshape(kc_ref.shape)
        vc_ref[...] = o_ref[:, v0:v0 + D_KV].reshape(vc_ref.shape)


def _inproj(x, norm_g, mod, w_in_bf16, kc, vc, layer):
    tm, tn = INPROJ_TM, INPROJ_TN
    assert COL_K // tn == (COL_V + D_KV - 1) // tn
    seqs = tm // SEQ
    last_ctx = N_CTX // tm - 1
    cache_spec = pl.BlockSpec((seqs, None, SEQ, D_KV), lambda i, j: (jnp.minimum(i, last_ctx), layer, 0, 0))
    cache_shape = jax.ShapeDtypeStruct((BATCH, DEPTH, SEQ, D_KV), F32)
    return pl.pallas_call(
        _inproj_kernel,
        grid=(N_TOK // tm, D_IN // tn),
        in_specs=[
            pl.BlockSpec((tm, D_MODEL), lambda i, j: (i, 0)),
            pl.BlockSpec((1, D_MODEL), lambda i, j: (0, 0)),
            _mod_spec(tm, 0),
            _mod_spec(tm, 1),
            pl.BlockSpec((None, D_MODEL, tn), lambda i, j: (layer, 0, j)),
            pl.BlockSpec(memory_space=pl.ANY),
            pl.BlockSpec(memory_space=pl.ANY),
        ],
        out_specs=[pl.BlockSpec((tm, tn), lambda i, j: (i, j)), cache_spec, cache_spec],
        out_shape=[jax.ShapeDtypeStruct((N_TOK, D_IN), F32), cache_shape, cache_shape],
        input_output_aliases={5: 1, 6: 2},
        scratch_shapes=[pltpu.VMEM((tm, D_MODEL), BF16)],
        compiler_params=_params("arbitrary", "arbitrary"),
        name="inproj",
    )(x, norm_g, mod, mod, w_in_bf16, kc, vc)


def _sink_column(sink_ref, kv_head, rows_per_head):
    n = KV_GROUP * rows_per_head
    row = lax.broadcasted_iota(jnp.int32, (n, 1), 0)
    col = jnp.full((n, 1), sink_ref[kv_head * KV_GROUP + KV_GROUP - 1], F32)
    for hh in range(KV_GROUP - 2, -1, -1):
        col = jnp.where(row < (hh + 1) * rows_per_head, sink_ref[kv_head * KV_GROUP + hh], col)
    return col


def _softmax_pv(scores, values, sink_col):
    m = sink_col
    for s in scores:
        m = jnp.maximum(m, jnp.max(s, axis=-1, keepdims=True))
    denom = jnp.exp(sink_col - m)
    out = None
    for s, v in zip(scores, values):
        p = jnp.exp(s - m)
        denom = denom + jnp.sum(p, axis=-1, keepdims=True)
        pv = _dot(p.astype(BF16), v)
        out = pv if out is None else out + pv
    return out / denom


def _stack_heads(q, kv_head):
    return jnp.concatenate(
        [q[:, (kv_head * KV_GROUP + hh) * HEAD_DIM:(kv_head * KV_GROUP + hh + 1) * HEAD_DIM]
         for hh in range(KV_GROUP)], axis=0)


def _ctx_attn_kernel(sink_ref, q_ref, kv_ref, o_ref):
    scale = HEAD_DIM ** -0.5
    q = q_ref[...]
    kv = kv_ref[...]
    for g in range(N_KV_HEADS):
        k = kv[:, g * HEAD_DIM:(g + 1) * HEAD_DIM].astype(BF16)
        v = kv[:, D_KV + g * HEAD_DIM:D_KV + (g + 1) * HEAD_DIM].astype(BF16)
        qs = _stack_heads(q, g).astype(BF16)
        s = _dot_nt(qs, k) * scale
        o = _softmax_pv([s], [v], _sink_column(sink_ref, g, SEQ))
        for hh in range(KV_GROUP):
            h = g * KV_GROUP + hh
            o_ref[:, h * HEAD_DIM:(h + 1) * HEAD_DIM] = o[hh * SEQ:(hh + 1) * SEQ].astype(o_ref.dtype)


def _ctx_attention(p, sink):
    kv_blk = 2 * D_KV
    return pl.pallas_call(
        _ctx_attn_kernel,
        grid_spec=pltpu.PrefetchScalarGridSpec(
            num_scalar_prefetch=1,
            grid=(BATCH,),
            in_specs=[
                pl.BlockSpec((SEQ, D_ATTN), lambda b, s: (b, 0)),
                pl.BlockSpec((SEQ, kv_blk), lambda b, s: (b, COL_K // kv_blk)),
            ],
            out_specs=pl.BlockSpec((SEQ, D_ATTN), lambda b, s: (b, 0)),
        ),
        out_shape=jax.ShapeDtypeStruct((N_TOK, D_ATTN), BF16),
        compiler_params=_params("arbitrary"),
        name="ctx_attention",
    )(sink, p, p)


def _rope(x, cos, sin_lo, sin_hi):
    return x * cos + pltpu.roll(x, 96, 1) * sin_lo + pltpu.roll(x, 32, 1) * sin_hi


def _lat_attn_kernel(sink_ref, q_ref, kv_ref, ck_ref, cv_ref, cos_ref, slo_ref, shi_ref, attn_hbm_ref,
                     o_ref, k_s, v_s, ck_s, cv_s):
    del attn_hbm_ref
    i = pl.program_id(1)
    scale = HEAD_DIM ** -0.5
    band = Q_BLOCK + 2 * WINDOW

    @pl.when(i == 0)
    def _():
        for g in range(N_KV_HEADS):
            sl = slice(g * HEAD_DIM, (g + 1) * HEAD_DIM)
            k = kv_ref[:, g * HEAD_DIM:(g + 1) * HEAD_DIM]
            k_s[:, sl] = _rope(k, cos_ref[...], slo_ref[...], shi_ref[...]).astype(BF16)
            v_s[:, sl] = kv_ref[:, D_KV + g * HEAD_DIM:D_KV + (g + 1) * HEAD_DIM].astype(BF16)
        ck_s[...] = ck_ref[...].astype(BF16)
        cv_s[...] = cv_ref[...].astype(BF16)

    q0 = pl.multiple_of(i * Q_BLOCK, Q_BLOCK)
    k0 = pl.multiple_of(jnp.clip(q0 - WINDOW, 0, DEC_SEQ - band), Q_BLOCK)
    cos = cos_ref[pl.ds(q0, Q_BLOCK), :]
    slo = slo_ref[pl.ds(q0, Q_BLOCK), :]
    shi = shi_ref[pl.ds(q0, Q_BLOCK), :]
    cos4 = jnp.concatenate([cos] * KV_GROUP, axis=0)
    slo4 = jnp.concatenate([slo] * KV_GROUP, axis=0)
    shi4 = jnp.concatenate([shi] * KV_GROUP, axis=0)

    n = KV_GROUP * Q_BLOCK
    qpos = q0 + (lax.broadcasted_iota(jnp.int32, (n, band), 0) & (Q_BLOCK - 1))
    kpos = k0 + lax.broadcasted_iota(jnp.int32, (n, band), 1)
    valid = jnp.abs(qpos - kpos) <= WINDOW

    q = q_ref[...]
    for g in range(N_KV_HEADS):
        sl = slice(g * HEAD_DIM, (g + 1) * HEAD_DIM)
        qs = _rope(_stack_heads(q, g), cos4, slo4, shi4).astype(BF16)
        kb = k_s[pl.ds(k0, band), sl]
        vb = v_s[pl.ds(k0, band), sl]
        s_band = jnp.where(valid, _dot_nt(qs, kb) * scale, NEG_INF)
        s_ctx = _dot_nt(qs, ck_s[:, sl]) * scale
        o = _softmax_pv([s_band, s_ctx], [vb, cv_s[:, sl]], _sink_column(sink_ref, g, Q_BLOCK))
        for hh in range(KV_GROUP):
            h = g * KV_GROUP + hh
            o_ref[:, h * HEAD_DIM:(h + 1) * HEAD_DIM] = (
                o[hh * Q_BLOCK:(hh + 1) * Q_BLOCK].astype(o_ref.dtype))


def _lat_attention(p, sink, cache_k, cache_v, rope_tabs, attn, layer):
    kv_blk = 2 * D_KV
    nq = DEC_SEQ // Q_BLOCK
    ctx_rows = N_CTX // Q_BLOCK
    seq_blk = N_CTX // DEC_SEQ
    tab_spec = pl.BlockSpec((DEC_SEQ, HEAD_DIM), lambda b, i, s: (0, 0))
    cache_spec = pl.BlockSpec((None, None, PAST_LEN, D_KV), lambda b, i, s: (b, layer, 0, 0))
    return pl.pallas_call(
        _lat_attn_kernel,
        grid_spec=pltpu.PrefetchScalarGridSpec(
            num_scalar_prefetch=1,
            grid=(DEC_BATCH, nq),
            in_specs=[
                pl.BlockSpec((Q_BLOCK, D_ATTN), lambda b, i, s: (ctx_rows + b * nq + i, 0)),
                pl.BlockSpec((DEC_SEQ, kv_blk), lambda b, i, s: (seq_blk + b, COL_K // kv_blk)),
                cache_spec, cache_spec, tab_spec, tab_spec, tab_spec,
                pl.BlockSpec(memory_space=pl.ANY),
            ],
            out_specs=pl.BlockSpec((Q_BLOCK, D_ATTN), lambda b, i, s: (ctx_rows + b * nq + i, 0)),
            scratch_shapes=[
                pltpu.VMEM((DEC_SEQ, D_KV), BF16), pltpu.VMEM((DEC_SEQ, D_KV), BF16),
                pltpu.VMEM((PAST_LEN, D_KV), BF16), pltpu.VMEM((PAST_LEN, D_KV), BF16),
            ],
        ),
        out_shape=jax.ShapeDtypeStruct((N_TOK, D_ATTN), BF16),
        input_output_aliases={8: 0},
        compiler_params=_params("arbitrary", "arbitrary"),
        name="lat_attention",
    )(sink, p, p, cache_k, cache_v, *rope_tabs, attn)


def _rope_tables():
    t = jnp.arange(DEC_SEQ)
    pos = jnp.stack([t // GRID_W, t % GRID_W], axis=-1).astype(F32)
    rd = HEAD_DIM // 4
    inv = ROPE_THETA ** (-jnp.arange(rd, dtype=F32) / rd)
    ang = jnp.broadcast_to(pos[:, :, None, None] * inv, (DEC_SEQ, 2, 2, rd)).reshape(DEC_SEQ, HEAD_DIM)
    cos, sin = jnp.cos(ang), jnp.sin(ang)
    first = (jnp.arange(HEAD_DIM) % (2 * rd)) < rd
    return cos, jnp.where(first, -sin, 0.0), jnp.where(first, 0.0, sin)


def _seq_position(chunk):
    ctx_per_seq = SEQ // CHUNK
    lat_per_seq = DEC_SEQ // CHUNK
    is_ctx = chunk < CTX_CHUNKS
    idx = jnp.where(is_ctx, chunk % ctx_per_seq, (chunk - CTX_CHUNKS) % lat_per_seq)
    per = jnp.where(is_ctx, ctx_per_seq, lat_per_seq)
    return idx == 0, idx == per - 1


def _lru_kernel(lx_ref, lg_ref, cw_ref, cb_ref, w4_ref, b4_ref, lam_ref, keep_ref, h0_ref,
                o_ref, hend_ref, a_s, b_s, tot_s, loc_s, hin_s):
    tile = pl.program_id(1)
    nch = LRU_TILE_CHUNKS
    row = lax.broadcasted_iota(jnp.int32, (CHUNK, LANES), 0)
    lam = lam_ref[...]
    c_log_sig = LRU_C * (jnp.minimum(lam, 0.0) - jnp.log(1.0 + jnp.exp(-jnp.abs(lam))))
    cw = cw_ref[...]
    cb = cb_ref[...]
    w4 = w4_ref[...]
    b4 = b4_ref[...]

    def gates(c, carry):
        r0 = pl.multiple_of(c * CHUNK, CHUNK)
        first, last = _seq_position(tile * nch + c)
        lo = pl.multiple_of(jnp.maximum(r0 - SUBLANES, 0), SUBLANES)
        hi = pl.multiple_of(jnp.minimum(r0 + CHUNK, LRU_TILE - SUBLANES), SUBLANES)
        ext = jnp.concatenate(
            [lx_ref[pl.ds(lo, SUBLANES), :], lx_ref[pl.ds(r0, CHUNK), :], lx_ref[pl.ds(hi, SUBLANES), :]],
            axis=0)
        xm2 = jnp.where(jnp.logical_and(first, row < 2), 0.0, ext[6:6 + CHUNK])
        xm1 = jnp.where(jnp.logical_and(first, row < 1), 0.0, ext[7:7 + CHUNK])
        x0 = ext[8:8 + CHUNK]
        xp1 = jnp.where(jnp.logical_and(last, row >= CHUNK - 1), 0.0, ext[9:9 + CHUNK])
        u = xm2 * cw[0:1] + xm1 * cw[1:2] + x0 * cw[2:3] + xp1 * cw[3:4] + cb
        g = _dot(u.astype(BF16), w4) + b4
        s0 = pl.multiple_of(c * CHUNK_PITCH, SUBLANES)
        for d in range(2):
            r = _sigmoid_tanh(g[:, (2 * d) * LANES:(2 * d + 1) * LANES])
            ig = _sigmoid_tanh(g[:, (2 * d + 1) * LANES:(2 * d + 2) * LANES])
            a = jnp.exp(r * c_log_sig[d:d + 1])
            a_s[d, pl.ds(s0, CHUNK), :] = a
            b_s[d, pl.ds(s0, CHUNK), :] = jnp.sqrt(1.0 - a * a) * ig * u
        return carry

    lax.fori_loop(0, nch, gates, 0, unroll=2)

    for d in range(2):
        def step(k, carry):
            acc_a, h = carry
            t = k if d == 0 else CHUNK - 1 - k
            idx = pl.ds(t, nch, stride=CHUNK_PITCH)
            a_t = a_s[d, idx, :]
            h = a_t * h + b_s[d, idx, :]
            acc_a = acc_a * a_t
            a_s[d, idx, :] = acc_a
            b_s[d, idx, :] = h
            return acc_a, h

        tot, loc = lax.fori_loop(
            0, CHUNK, step, (jnp.ones((nch, LANES), F32), jnp.zeros((nch, LANES), F32)))
        tot_s[d] = tot
        loc_s[d] = loc

    for d in range(2):
        prev = jnp.zeros((1, LANES), F32)
        order = range(nch) if d == 0 else range(nch - 1, -1, -1)
        for c in order:
            hin = keep_ref[d, c:c + 1, :] * prev + h0_ref[d, c:c + 1, :]
            hin_s[d, c:c + 1, :] = hin
            prev = tot_s[d, c:c + 1, :] * hin + loc_s[d, c:c + 1, :]
            hend_ref[d, c:c + 1, :] = prev

    def emit(c, carry):
        r0 = pl.multiple_of(c * CHUNK, CHUNK)
        s0 = pl.multiple_of(c * CHUNK_PITCH, SUBLANES)
        y = None
        for d in range(2):
            h = b_s[d, pl.ds(s0, CHUNK), :] + a_s[d, pl.ds(s0, CHUNK), :] * hin_s[d, pl.ds(c, 1), :]
            y = h if y is None else y + h
        o_ref[pl.ds(r0, CHUNK), :] = (y * _gelu_tanh(lg_ref[pl.ds(r0, CHUNK), :])).astype(o_ref.dtype)
        return carry

    lax.fori_loop(0, nch, emit, 0, unroll=2)


def _lru(p, conv_w, conv_b, w4, b4, lam, keep, h0):
    nt = N_TOK // LRU_TILE
    nch = LRU_TILE_CHUNKS
    lx_blk = COL_LX // LRU_BLK
    lg_blk = COL_LG // LRU_BLK
    return pl.pallas_call(
        _lru_kernel,
        grid=(LRU_BLOCKS, nt),
        in_specs=[
            pl.BlockSpec((LRU_TILE, LRU_BLK), lambda n, r: (r, lx_blk + n)),
            pl.BlockSpec((LRU_TILE, LRU_BLK), lambda n, r: (r, lg_blk + n)),
            pl.BlockSpec((CONV_W, LRU_BLK), lambda n, r: (0, n)),
            pl.BlockSpec((1, LRU_BLK), lambda n, r: (0, n)),
            pl.BlockSpec((None, LRU_BLK, 4 * LRU_BLK), lambda n, r: (n, 0, 0)),
            pl.BlockSpec((None, 1, 4 * LRU_BLK), lambda n, r: (n, 0, 0)),
            pl.BlockSpec((2, LRU_BLK), lambda n, r: (0, n)),
            pl.BlockSpec((2, nch, LRU_BLK), lambda n, r: (0, r, 0)),
            pl.BlockSpec((2, nch, LRU_BLK), lambda n, r: (0, r, n)),
        ],
        out_specs=[
            pl.BlockSpec((LRU_TILE, LRU_BLK), lambda n, r: (r, n)),
            pl.BlockSpec((2, nch, LRU_BLK), lambda n, r: (0, r, n)),
        ],
        out_shape=[
            jax.ShapeDtypeStruct((N_TOK, D_LRU), BF16),
            jax.ShapeDtypeStruct((2, N_CHUNKS, D_LRU), F32),
        ],
        scratch_shapes=[
            pltpu.VMEM((2, nch * CHUNK_PITCH, LANES), F32),
            pltpu.VMEM((2, nch * CHUNK_PITCH, LANES), F32),
            pltpu.VMEM((2, nch, LANES), F32),
            pltpu.VMEM((2, nch, LANES), F32),
            pltpu.VMEM((2, nch, LANES), F32),
        ],
        compiler_params=_params("arbitrary", "arbitrary"),
        name="rglru",
    )(p, p, conv_w, conv_b, w4, b4, lam, keep, h0)


def _lru_chain_inputs(state_l):
    c = jnp.arange(N_CHUNKS)
    ctx_per_seq = SEQ // CHUNK
    lat_per_seq = DEC_SEQ // CHUNK
    is_ctx = c < CTX_CHUNKS
    idx = jnp.where(is_ctx, c % ctx_per_seq, (c - CTX_CHUNKS) % lat_per_seq)
    per = jnp.where(is_ctx, ctx_per_seq, lat_per_seq)
    start = jnp.stack([idx == 0, idx == per - 1])
    keep = jnp.broadcast_to(jnp.where(start, 0.0, 1.0)[:, :, None], (2, N_CHUNKS, LANES)).astype(F32)
    lat_b = jnp.clip((c - CTX_CHUNKS) // lat_per_seq, 0, DEC_BATCH - 1)
    seed = jnp.transpose(state_l, (1, 0, 2))[:, lat_b, :]
    h0 = jnp.where((start & ~is_ctx[None, :])[:, :, None], seed, 0.0).astype(F32)
    return keep, h0


def _pool_kernel(x_ref, w_ref, sc_ref, o_ref):
    tile = pl.program_id(0)
    ext_rows = POOL_CHUNK + 2 * POOL_HALO
    is_ctx = tile * POOL_TILE < N_CTX
    erow = lax.broadcasted_iota(jnp.int32, (ext_rows, LANES), 0)
    row = lax.broadcasted_iota(jnp.int32, (POOL_CHUNK, LANES), 0)

    def shift_up(x, k):
        return x if k == 0 else pltpu.roll(x, ext_rows - k, 0)

    for g, win in enumerate(POOL_WINDOWS):
        half = win // 2
        cols = slice(g * POOL_GROUP, (g + 1) * POOL_GROUP)
        w = w_ref[g]
        sc = sc_ref[:, cols]

        def body(c, carry):
            r0 = pl.multiple_of(c * POOL_CHUNK, POOL_CHUNK)
            seq_len = jnp.where(is_ctx, SEQ, DEC_SEQ)
            t0 = jnp.where(is_ctx, 0, (tile * POOL_TILE + r0) & (DEC_SEQ - 1))
            first = t0 == 0
            last = t0 + POOL_CHUNK == seq_len
            lo = pl.multiple_of(jnp.maximum(r0 - POOL_HALO, 0), POOL_HALO)
            hi = pl.multiple_of(jnp.minimum(r0 + POOL_CHUNK, POOL_TILE - POOL_HALO), POOL_HALO)
            cur = x_ref[pl.ds(r0, POOL_CHUNK), cols]
            ext = jnp.concatenate(
                [x_ref[pl.ds(lo, POOL_HALO), cols], cur, x_ref[pl.ds(hi, POOL_HALO), cols]], axis=0)
            outside = jnp.logical_or(jnp.logical_and(first, erow < POOL_HALO),
                                     jnp.logical_and(last, erow >= POOL_HALO + POOL_CHUNK))
            acc = jnp.where(outside, 0.0, ext)
            span = 1
            while span < win:
                acc = acc + shift_up(acc, span)
                span *= 2
            total = shift_up(acc, POOL_HALO - half)[:POOL_CHUNK]
            t = t0 + row
            cnt = jnp.minimum(t + half, seq_len) - jnp.maximum(t - half, 0)
            pooled = total / cnt.astype(F32) - cur
            y = _dot(pooled.astype(BF16), w) * sc
            o_ref[pl.ds(r0, POOL_CHUNK), cols] = y.astype(o_ref.dtype)
            return carry

        lax.fori_loop(0, POOL_TILE // POOL_CHUNK, body, 0)


def _pool(p, pool_w_bf16, pool_scale):
    blk = COL_PU // D_POOL
    return pl.pallas_call(
        _pool_kernel,
        grid=(N_TOK // POOL_TILE,),
        in_specs=[
            pl.BlockSpec((POOL_TILE, D_POOL), lambda i: (i, blk)),
            pl.BlockSpec((len(POOL_WINDOWS), POOL_GROUP, POOL_GROUP), lambda i: (0, 0, 0)),
            pl.BlockSpec((1, D_POOL), lambda i: (0, 0)),
        ],
        out_specs=pl.BlockSpec((POOL_TILE, D_POOL), lambda i: (i, 0)),
        out_shape=jax.ShapeDtypeStruct((N_TOK, D_POOL), BF16),
        compiler_params=_params("arbitrary"),
        name="pool_mixer",
    )(p, pool_w_bf16, pool_scale)


OUTPROJ_TM = 1024
OUTPROJ_TN = 1024


def _outproj_kernel(attn_ref, lru_ref, pool_ref, x_ref, gate_ref, wa_ref, wl_ref, wp_ref, o_ref):
    mix = _dot(attn_ref[...], wa_ref[...]) + _dot(lru_ref[...], wl_ref[...]) + _dot(pool_ref[...], wp_ref[...])
    o_ref[...] = x_ref[...] + gate_ref[...] * mix


def _outproj(attn, lru, pool, x, mod, w_out_bf16, layer):
    tm, tn = OUTPROJ_TM, OUTPROJ_TN
    return pl.pallas_call(
        _outproj_kernel,
        grid=(N_TOK // tm, D_MODEL // tn),
        in_specs=[
            pl.BlockSpec((tm, D_ATTN), lambda i, j: (i, 0)),
            pl.BlockSpec((tm, D_LRU), lambda i, j: (i, 0)),
            pl.BlockSpec((tm, D_POOL), lambda i, j: (i, 0)),
            pl.BlockSpec((tm, tn), lambda i, j: (i, j)),
            _mod_spec(tm, 2, tn),
            pl.BlockSpec((None, D_ATTN, tn), lambda i, j: (layer, 0, j)),
            pl.BlockSpec((None, D_LRU, tn), lambda i, j: (layer, D_ATTN // D_LRU, j)),
            pl.BlockSpec((None, D_POOL, tn), lambda i, j: (layer, (D_ATTN + D_LRU) // D_POOL, j)),
        ],
        out_specs=pl.BlockSpec((tm, tn), lambda i, j: (i, j)),
        out_shape=jax.ShapeDtypeStruct((N_TOK, D_MODEL), F32),
        compiler_params=_params("arbitrary", "arbitrary"),
        name="outproj",
    )(attn, lru, pool, x, mod, w_out_bf16, w_out_bf16, w_out_bf16)


FFN_TM = 1024
FFN_TF = 512
FFN_TN = 512


def _ffn_kernel(x_ref, g_ref, shift_ref, scale_ref, gate_ref, wg_ref, wu_ref, w2_ref, fg_ref, o_ref, h_ref,
                *, final_norm):
    j = pl.program_id(1)

    @pl.when(j == 0)
    def _():
        _norm_mod_store(h_ref, x_ref, g_ref[...], scale_ref[...], shift_ref[...], FFN_TM, copy_ref=o_ref)

    h = h_ref[...]
    act = (_silu(_dot(h, wg_ref[...])) * _dot(h, wu_ref[...])).astype(BF16)
    for n in range(D_MODEL // FFN_TN):
        cols = slice(n * FFN_TN, (n + 1) * FFN_TN)
        o_ref[:, cols] += gate_ref[:, cols] * _dot(act, w2_ref[:, cols])

    if final_norm:
        @pl.when(j == pl.num_programs(1) - 1)
        def _():
            gain = fg_ref[...]

            def body(r, carry):
                r0 = pl.multiple_of(r * NORM_ROWS, NORM_ROWS)
                x = o_ref[pl.ds(r0, NORM_ROWS), :]
                ms = jnp.mean(x * x, axis=-1, keepdims=True)
                o_ref[pl.ds(r0, NORM_ROWS), :] = x * lax.rsqrt(ms + RMS_EPS) * gain
                return carry

            lax.fori_loop(0, FFN_TM // NORM_ROWS, body, 0, unroll=NORM_UNROLL)


def _ffn(x, norm_g, mod, w1_bf16, w2_bf16, final_g, layer, tile0, ntiles, final_norm):
    tm, tf = FFN_TM, FFN_TF
    nf = D_FF // tf

    def mod_spec(which):
        return pl.BlockSpec((None, 1, D_MODEL),
                            lambda i, j: (_group_of_tile(tile0 + i, tm) * N_MOD + which, 0, 0))

    return pl.pallas_call(
        functools.partial(_ffn_kernel, final_norm=final_norm),
        grid=(ntiles, nf),
        in_specs=[
            pl.BlockSpec((tm, D_MODEL), lambda i, j: (tile0 + i, 0), pipeline_mode=pl.Buffered(1)),
            pl.BlockSpec((1, D_MODEL), lambda i, j: (0, 0)),
            mod_spec(3),
            mod_spec(4),
            mod_spec(5),
            pl.BlockSpec((None, D_MODEL, tf), lambda i, j: (layer, 0, j)),
            pl.BlockSpec((None, D_MODEL, tf), lambda i, j: (layer, 0, nf + j)),
            pl.BlockSpec((None, tf, D_MODEL), lambda i, j: (layer, j, 0)),
            pl.BlockSpec((1, D_MODEL), lambda i, j: (0, 0)),
        ],
        out_specs=pl.BlockSpec((tm, D_MODEL), lambda i, j: (i, 0)),
        out_shape=jax.ShapeDtypeStruct((ntiles * tm, D_MODEL), F32),
        scratch_shapes=[pltpu.VMEM((tm, D_MODEL), BF16)],
        compiler_params=_params("arbitrary", "arbitrary"),
        name="ffn_final" if final_norm else "ffn",
    )(x, norm_g, mod, mod, mod, w1_bf16, w1_bf16, w2_bf16, final_g)


def kernel(x_prompt, x_sample, cache_k, cache_v, state_lru, c, c_ctx, mod_w, mod_b, norm_mix, norm_ffn,
           w_in, attn_sink, conv_w, conv_b, lru_wa, lru_ba, lru_wx, lru_bx, lru_lambda, pool_w,
           pool_scale, w_out, ffn_w1, ffn_w2, norm_final):
    x = jnp.concatenate([x_prompt.reshape(N_CTX, D_MODEL), x_sample.reshape(N_LAT, D_MODEL)], axis=0)

    cond = jnp.zeros((COND_ROWS, D_MODEL), F32).at[0].set(c_ctx).at[1:1 + DEC_BATCH].set(c)
    mod_all = _mod_vectors(cond, mod_w, mod_b)
    mod_all = mod_all[:, :N_GROUPS].reshape(DEPTH, N_GROUPS * N_MOD, 1, D_MODEL)

    w_in_b = w_in.astype(BF16)
    w_out_b = w_out.astype(BF16)
    w1_b = ffn_w1.astype(BF16)
    w2_b = ffn_w2.astype(BF16)
    pool_w_b = pool_w.astype(BF16)
    w4 = jnp.concatenate([lru_wa[:, 0], lru_wx[:, 0], lru_wa[:, 1], lru_wx[:, 1]], axis=-1).astype(BF16)
    ba = lru_ba.reshape(DEPTH, 2, LRU_BLOCKS, LRU_BLK)
    bx = lru_bx.reshape(DEPTH, 2, LRU_BLOCKS, LRU_BLK)
    b4 = jnp.stack([ba[:, 0], bx[:, 0], ba[:, 1], bx[:, 1]], axis=2).reshape(DEPTH, LRU_BLOCKS, 1, 4 * LRU_BLK)
    cache_k2 = cache_k.reshape(DEC_BATCH, DEPTH, PAST_LEN, D_KV)
    cache_v2 = cache_v.reshape(DEC_BATCH, DEPTH, PAST_LEN, D_KV)
    rope_tabs = _rope_tables()

    kc = jnp.zeros((BATCH, DEPTH, SEQ, D_KV), F32)
    vc = jnp.zeros((BATCH, DEPTH, SEQ, D_KV), F32)
    hs_new = []
    for l in range(DEPTH):
        mod = mod_all[l]
        p, kc, vc = _inproj(x, norm_mix[l][None, :], mod, w_in_b, kc, vc, l)

        attn = _ctx_attention(p, attn_sink[l])
        attn = _lat_attention(p, attn_sink[l], cache_k2, cache_v2, rope_tabs, attn, l)
        keep, h0 = _lru_chain_inputs(state_lru[:, l])
        lru, hend = _lru(p, conv_w[l], conv_b[l][None, :], w4[l], b4[l], lru_lambda[l], keep, h0)
        pool = _pool(p, pool_w_b[l], pool_scale[l][None, :])

        ctx_per_seq = SEQ // CHUNK
        fwd_fin = hend[0, ctx_per_seq - 1:CTX_CHUNKS:ctx_per_seq]
        bwd_fin = hend[1, 0:CTX_CHUNKS:ctx_per_seq]
        hs_new.append(jnp.stack([fwd_fin, bwd_fin], axis=1))

        x = _outproj(attn, lru, pool, x, mod, w_out_b, l)
        ffn_args = (x, norm_ffn[l][None, :], mod, w1_b, w2_b, norm_final[None, :], l)
        if l < DEPTH - 1:
            x = _ffn(*ffn_args, 0, N_TOK // FFN_TM, False)

    y_prompt = _ffn(*ffn_args, 0, N_CTX // FFN_TM, True).reshape(BATCH, SEQ, D_MODEL)
    y_sample = _ffn(*ffn_args, N_CTX // FFN_TM, N_LAT // FFN_TM, True).reshape(DEC_BATCH, DEC_SEQ, D_MODEL)
    cache_shape = (BATCH, DEPTH, SEQ, N_KV_HEADS, HEAD_DIM)
    return (y_prompt, y_sample, kc.reshape(cache_shape), vc.reshape(cache_shape), jnp.stack(hs_new, axis=1))
```

```python
import functools
import math

import jax
import jax.numpy as jnp
from jax import lax
from jax.experimental import pallas as pl
from jax.experimental.pallas import tpu as pltpu

F32 = jnp.float32
BF16 = jnp.bfloat16

D_MODEL = 2048
BATCH = 32
SEQ = 256
DEPTH = 4
DEC_BATCH = 2
DEC_SEQ = 2048
PAST_LEN = 512
GRID_W = 64
HEAD_DIM = 128
N_HEADS = 8
N_KV_HEADS = 2
KV_GROUP = N_HEADS // N_KV_HEADS
D_ATTN = N_HEADS * HEAD_DIM
D_KV = N_KV_HEADS * HEAD_DIM
WINDOW = 128
Q_BLOCK = 128
ROPE_THETA = 10000.0
D_LRU = D_MODEL // 4
LRU_BLOCKS = 4
LRU_BLK = D_LRU // LRU_BLOCKS
CONV_W = 4
LRU_C = 8.0
D_POOL = D_MODEL // 4
POOL_WINDOWS = (2, 4, 8, 16)
POOL_GROUP = D_POOL // len(POOL_WINDOWS)
D_IN = D_ATTN + 2 * D_KV + 2 * D_LRU + D_POOL
D_FF = 5632
N_MOD = 6
RMS_EPS = 1e-6
NEG_INF = -1e30

N_CTX = BATCH * SEQ
N_LAT = DEC_BATCH * DEC_SEQ
N_TOK = N_CTX + N_LAT
N_GROUPS = 1 + DEC_BATCH
COND_ROWS = 16

COL_Q = 0
COL_K = D_ATTN
COL_V = D_ATTN + D_KV
COL_LX = D_ATTN + 2 * D_KV
COL_LG = COL_LX + D_LRU
COL_PU = COL_LG + D_LRU

VMEM_LIMIT = 56 * 1024 * 1024

NORM_ROWS = 32
NORM_UNROLL = 4
LANES = 128
SUBLANES = 8

CHUNK = 128
CHUNK_PITCH = 136
LRU_TILE_CHUNKS = 48
LRU_TILE = LRU_TILE_CHUNKS * CHUNK
N_CHUNKS = N_TOK // CHUNK
CTX_CHUNKS = N_CTX // CHUNK

POOL_TILE = DEC_SEQ
POOL_CHUNK = 256
POOL_HALO = 8


def _dot(a, b):
    return jnp.dot(a, b, preferred_element_type=F32)


def _dot_nt(a, b):
    return lax.dot_general(a, b, (((1,), (1,)), ((), ())), preferred_element_type=F32)


def _sigmoid(x):
    return 1.0 / (1.0 + jnp.exp(-x))


def _sigmoid_tanh(x):
    return 0.5 * jnp.tanh(0.5 * x) + 0.5


def _silu(x):
    return x * _sigmoid(x)


def _gelu_tanh(x):
    c = math.sqrt(2.0 / math.pi)
    return 0.5 * x * (1.0 + jnp.tanh(c * (x + 0.044715 * (x * x * x))))


def _params(*sem):
    return pltpu.CompilerParams(dimension_semantics=sem, vmem_limit_bytes=VMEM_LIMIT)


CAST_TILES = 8
CAST_ROWS = D_MODEL // CAST_TILES


def _cast_idx(i, j, j_last):
    active = i < CAST_TILES
    return jnp.minimum(i, CAST_TILES - 1), jnp.where(active, jnp.minimum(j, j_last), j_last)


def _group_of_tile(i, tm):
    n_ctx_tiles = N_CTX // tm
    tiles_per_lat = DEC_SEQ // tm
    return jnp.where(i < n_ctx_tiles, 0, 1 + (i - n_ctx_tiles) // tiles_per_lat)


def _mod_spec(tm, which):
    return pl.BlockSpec((None, 1, D_MODEL), lambda i, j: (_group_of_tile(i, tm) * N_MOD + which, 0, 0))


def _norm_mod_store(h_ref, x_ref, g, scale, shift, rows, copy_ref=None):
    gain = g * (1.0 + scale)

    def body(r, carry):
        r0 = pl.multiple_of(r * NORM_ROWS, NORM_ROWS)
        x = x_ref[pl.ds(r0, NORM_ROWS), :]
        ms = jnp.mean(x * x, axis=-1, keepdims=True)
        h_ref[pl.ds(r0, NORM_ROWS), :] = (x * lax.rsqrt(ms + RMS_EPS) * gain + shift).astype(h_ref.dtype)
        if copy_ref is not None:
            copy_ref[pl.ds(r0, NORM_ROWS), :] = x
        return carry

    lax.fori_loop(0, rows // NORM_ROWS, body, 0, unroll=NORM_UNROLL)


MOD_TN = 1024


def _mod_kernel(cond_ref, w_ref, b_ref, o_ref):
    s = _silu(cond_ref[...]).astype(BF16)
    o_ref[...] = _dot(s, w_ref[...].astype(BF16)) + b_ref[...]


def _mod_vectors(cond, mod_w, mod_b):
    n = N_MOD * D_MODEL
    return pl.pallas_call(
        _mod_kernel,
        grid=(DEPTH, n // MOD_TN),
        in_specs=[
            pl.BlockSpec((COND_ROWS, D_MODEL), lambda l, j: (0, 0)),
            pl.BlockSpec((None, D_MODEL, MOD_TN), lambda l, j: (l, 0, j)),
            pl.BlockSpec((None, 1, MOD_TN), lambda l, j: (l, 0, j)),
        ],
        out_specs=pl.BlockSpec((None, COND_ROWS, MOD_TN), lambda l, j: (l, 0, j)),
        out_shape=jax.ShapeDtypeStruct((DEPTH, COND_ROWS, n), F32),
        compiler_params=_params("arbitrary", "arbitrary"),
        name="mod_vectors",
    )(cond, mod_w, mod_b.reshape(DEPTH, 1, n))


INPROJ_TM = 1024
INPROJ_TN = 1024


def _inproj_kernel(x_ref, g_ref, shift_ref, scale_ref, w_ref, kc_in_ref, vc_in_ref, *rest, cast_next):
    del kc_in_ref, vc_in_ref
    if cast_next:
        win_f32_ref, wout_f32_ref, o_ref, kc_ref, vc_ref, win_ref, wout_ref, h_ref = rest
    else:
        o_ref, kc_ref, vc_ref, h_ref = rest
    i, j = pl.program_id(0), pl.program_id(1)

    @pl.when(j == 0)
    def _():
        _norm_mod_store(h_ref, x_ref, g_ref[...], scale_ref[...], shift_ref[...], INPROJ_TM)

    o_ref[...] = _dot(h_ref[...], w_ref[...])

    @pl.when(jnp.logical_and(j == COL_K // INPROJ_TN, i < N_CTX // INPROJ_TM))
    def _():
        k0 = COL_K % INPROJ_TN
        v0 = COL_V % INPROJ_TN
        kc_ref[...] = o_ref[:, k0:k0 + D_KV].reshape(kc_ref.shape)
        vc_ref[...] = o_ref[:, v0:v0 + D_KV].reshape(vc_ref.shape)

    if cast_next:
        @pl.when(i < CAST_TILES)
        def _():
            win_ref[...] = win_f32_ref[...].astype(BF16)

        @pl.when(jnp.logical_and(i < CAST_TILES, j < D_MODEL // INPROJ_TN))
        def _():
            wout_ref[...] = wout_f32_ref[...].astype(BF16)


def _inproj(x, norm_g, mod, w_in_l, kc, vc, layer, w_in_f32=None, w_out_f32=None):
    tm, tn = INPROJ_TM, INPROJ_TN
    assert COL_K // tn == (COL_V + D_KV - 1) // tn
    cast_next = w_in_f32 is not None
    seqs = tm // SEQ
    last_ctx = N_CTX // tm - 1
    nj = D_IN // tn
    cache_spec = pl.BlockSpec((seqs, None, SEQ, D_KV), lambda i, j: (jnp.minimum(i, last_ctx), layer, 0, 0))
    cache_shape = jax.ShapeDtypeStruct((BATCH, DEPTH, SEQ, D_KV), F32)
    in_specs = [
        pl.BlockSpec((tm, D_MODEL), lambda i, j: (i, 0)),
        pl.BlockSpec((1, D_MODEL), lambda i, j: (0, 0)),
        _mod_spec(tm, 0),
        _mod_spec(tm, 1),
        pl.BlockSpec((D_MODEL, tn), lambda i, j: (0, j)),
        pl.BlockSpec(memory_space=pl.ANY),
        pl.BlockSpec(memory_space=pl.ANY),
    ]
    out_specs = [pl.BlockSpec((tm, tn), lambda i, j: (i, j)), cache_spec, cache_spec]
    out_shape = [jax.ShapeDtypeStruct((N_TOK, D_IN), F32), cache_shape, cache_shape]
    args = [x, norm_g, mod, mod, w_in_l, kc, vc]
    if cast_next:
        nxt = layer + 1
        win_idx = lambda i, j: _cast_idx(i, j, nj - 1)
        wout_idx = lambda i, j: _cast_idx(i, j, D_MODEL // tn - 1)
        in_specs += [
            pl.BlockSpec((None, CAST_ROWS, tn), lambda i, j: (nxt,) + win_idx(i, j)),
            pl.BlockSpec((None, CAST_ROWS, tn), lambda i, j: (nxt,) + wout_idx(i, j)),
        ]
        out_specs += [pl.BlockSpec((CAST_ROWS, tn), win_idx), pl.BlockSpec((CAST_ROWS, tn), wout_idx)]
        out_shape += [jax.ShapeDtypeStruct((D_MODEL, D_IN), BF16), jax.ShapeDtypeStruct((D_MODEL, D_MODEL), BF16)]
        args += [w_in_f32, w_out_f32]
    return pl.pallas_call(
        functools.partial(_inproj_kernel, cast_next=cast_next),
        grid=(N_TOK // tm, nj),
        in_specs=in_specs,
        out_specs=out_specs,
        out_shape=out_shape,
        input_output_aliases={5: 1, 6: 2},
        scratch_shapes=[pltpu.VMEM((tm, D_MODEL), BF16)],
        compiler_params=_params("arbitrary", "arbitrary"),
        name="inproj_cast" if cast_next else "inproj",
    )(*args)


def _sink_column(sink_ref, kv_head, rows_per_head):
    n = KV_GROUP * rows_per_head
    row = lax.broadcasted_iota(jnp.int32, (n, 1), 0)
    col = jnp.full((n, 1), sink_ref[kv_head * KV_GROUP + KV_GROUP - 1], F32)
    for hh in range(KV_GROUP - 2, -1, -1):
        col = jnp.where(row < (hh + 1) * rows_per_head, sink_ref[kv_head * KV_GROUP + hh], col)
    return col


def _softmax_pv(scores, values, sink_col):
    m = sink_col
    for s in scores:
        m = jnp.maximum(m, jnp.max(s, axis=-1, keepdims=True))
    denom = jnp.exp(sink_col - m)
    out = None
    for s, v in zip(scores, values):
        p = jnp.exp(s - m)
        denom = denom + jnp.sum(p, axis=-1, keepdims=True)
        pv = _dot(p.astype(BF16), v)
        out = pv if out is None else out + pv
    return out / denom


def _stack_heads(q, kv_head):
    return jnp.concatenate(
        [q[:, (kv_head * KV_GROUP + hh) * HEAD_DIM:(kv_head * KV_GROUP + hh + 1) * HEAD_DIM]
         for hh in range(KV_GROUP)], axis=0)


def _ctx_attn_kernel(sink_ref, q_ref, kv_ref, buf_hbm_ref, o_ref):
    del buf_hbm_ref
    scale = HEAD_DIM ** -0.5
    q = q_ref[...]
    kv = kv_ref[...]
    for g in range(N_KV_HEADS):
        k = kv[:, g * HEAD_DIM:(g + 1) * HEAD_DIM].astype(BF16)
        v = kv[:, D_KV + g * HEAD_DIM:D_KV + (g + 1) * HEAD_DIM].astype(BF16)
        qs = _stack_heads(q, g).astype(BF16)
        s = _dot_nt(qs, k) * scale
        o = _softmax_pv([s], [v], _sink_column(sink_ref, g, SEQ))
        for hh in range(KV_GROUP):
            h = g * KV_GROUP + hh
            o_ref[:, h * HEAD_DIM:(h + 1) * HEAD_DIM] = o[hh * SEQ:(hh + 1) * SEQ].astype(o_ref.dtype)


def _ctx_attention(p, sink, buf):
    kv_blk = 2 * D_KV
    return pl.pallas_call(
        _ctx_attn_kernel,
        grid_spec=pltpu.PrefetchScalarGridSpec(
            num_scalar_prefetch=1,
            grid=(BATCH,),
            in_specs=[
                pl.BlockSpec((SEQ, D_ATTN), lambda b, s: (b, 0)),
                pl.BlockSpec((SEQ, kv_blk), lambda b, s: (b, COL_K // kv_blk)),
                pl.BlockSpec(memory_space=pl.ANY),
            ],
            out_specs=pl.BlockSpec((SEQ, D_ATTN), lambda b, s: (b, 0)),
        ),
        out_shape=jax.ShapeDtypeStruct((N_TOK, D_ATTN), BF16),
        input_output_aliases={3: 0},
        compiler_params=_params("arbitrary"),
        name="ctx_attention",
    )(sink, p, p, buf)


def _rope(x, cos, sin_lo, sin_hi):
    return x * cos + pltpu.roll(x, 96, 1) * sin_lo + pltpu.roll(x, 32, 1) * sin_hi


def _lat_attn_kernel(sink_ref, q_ref, kv_ref, ck_ref, cv_ref, cos_ref, slo_ref, shi_ref, attn_hbm_ref,
                     o_ref, kt_s, v_s, ckt_s, cv_s):
    del attn_hbm_ref
    i = pl.program_id(1)
    scale = HEAD_DIM ** -0.5
    band = Q_BLOCK + 2 * WINDOW
    band_blocks = band // Q_BLOCK
    n_keys = band + PAST_LEN

    @pl.when(i == 0)
    def _():
        for g in range(N_KV_HEADS):
            sl = slice(g * HEAD_DIM, (g + 1) * HEAD_DIM)
            k = _rope(kv_ref[:, sl], cos_ref[...], slo_ref[...], shi_ref[...])
            for t in range(DEC_SEQ // Q_BLOCK):
                kt_s[g, t] = k[t * Q_BLOCK:(t + 1) * Q_BLOCK].T.astype(BF16)
            ckt_s[g] = ck_ref[:, sl].T.astype(BF16)
        v_s[...] = kv_ref[:, D_KV:2 * D_KV].astype(BF16)
        cv_s[...] = cv_ref[...].astype(BF16)

    q0 = pl.multiple_of(i * Q_BLOCK, Q_BLOCK)
    kb0 = jnp.clip(i - WINDOW // Q_BLOCK, 0, DEC_SEQ // Q_BLOCK - band_blocks)
    k0 = pl.multiple_of(kb0 * Q_BLOCK, Q_BLOCK)
    cos = cos_ref[pl.ds(q0, Q_BLOCK), :] * scale
    slo = slo_ref[pl.ds(q0, Q_BLOCK), :] * scale
    shi = shi_ref[pl.ds(q0, Q_BLOCK), :] * scale
    cos4 = jnp.concatenate([cos] * KV_GROUP, axis=0)
    slo4 = jnp.concatenate([slo] * KV_GROUP, axis=0)
    shi4 = jnp.concatenate([shi] * KV_GROUP, axis=0)

    n = KV_GROUP * Q_BLOCK
    qpos = q0 + (lax.broadcasted_iota(jnp.int32, (n, n_keys), 0) & (Q_BLOCK - 1))
    col = lax.broadcasted_iota(jnp.int32, (n, n_keys), 1)
    valid = jnp.logical_or(col >= band, jnp.abs(qpos - (k0 + col)) <= WINDOW)

    q = q_ref[...]
    for g in range(N_KV_HEADS):
        sl = slice(g * HEAD_DIM, (g + 1) * HEAD_DIM)
        qs = _rope(_stack_heads(q, g), cos4, slo4, shi4).astype(BF16)
        kt = jnp.concatenate([kt_s[g, kb0 + t] for t in range(band_blocks)] + [ckt_s[g]], axis=1)
        v = jnp.concatenate([v_s[pl.ds(k0, band), sl], cv_s[:, sl]], axis=0)
        s = jnp.where(valid, _dot(qs, kt), NEG_INF)
        o = _softmax_pv([s], [v], _sink_column(sink_ref, g, Q_BLOCK))
        for hh in range(KV_GROUP):
            h = g * KV_GROUP + hh
            o_ref[:, h * HEAD_DIM:(h + 1) * HEAD_DIM] = (
                o[hh * Q_BLOCK:(hh + 1) * Q_BLOCK].astype(o_ref.dtype))


def _lat_attention(p, sink, cache_k, cache_v, rope_tabs, attn, layer):
    kv_blk = 2 * D_KV
    nq = DEC_SEQ // Q_BLOCK
    ctx_rows = N_CTX // Q_BLOCK
    seq_blk = N_CTX // DEC_SEQ
    tab_spec = pl.BlockSpec((DEC_SEQ, HEAD_DIM), lambda b, i, s: (0, 0))
    cache_spec = pl.BlockSpec((None, None, PAST_LEN, D_KV), lambda b, i, s: (b, layer, 0, 0))
    return pl.pallas_call(
        _lat_attn_kernel,
        grid_spec=pltpu.PrefetchScalarGridSpec(
            num_scalar_prefetch=1,
            grid=(DEC_BATCH, nq),
            in_specs=[
                pl.BlockSpec((Q_BLOCK, D_ATTN), lambda b, i, s: (ctx_rows + b * nq + i, 0)),
                pl.BlockSpec((DEC_SEQ, kv_blk), lambda b, i, s: (seq_blk + b, COL_K // kv_blk)),
                cache_spec, cache_spec, tab_spec, tab_spec, tab_spec,
                pl.BlockSpec(memory_space=pl.ANY),
            ],
            out_specs=pl.BlockSpec((Q_BLOCK, D_ATTN), lambda b, i, s: (ctx_rows + b * nq + i, 0)),
            scratch_shapes=[
                pltpu.VMEM((N_KV_HEADS, DEC_SEQ // Q_BLOCK, HEAD_DIM, Q_BLOCK), BF16),
                pltpu.VMEM((DEC_SEQ, D_KV), BF16),
                pltpu.VMEM((N_KV_HEADS, HEAD_DIM, PAST_LEN), BF16),
                pltpu.VMEM((PAST_LEN, D_KV), BF16),
            ],
        ),
        out_shape=jax.ShapeDtypeStruct((N_TOK, D_ATTN), BF16),
        input_output_aliases={8: 0},
        compiler_params=_params("arbitrary", "arbitrary"),
        name="lat_attention",
    )(sink, p, p, cache_k, cache_v, *rope_tabs, attn)


def _rope_tables():
    t = jnp.arange(DEC_SEQ)
    pos = jnp.stack([t // GRID_W, t % GRID_W], axis=-1).astype(F32)
    rd = HEAD_DIM // 4
    inv = ROPE_THETA ** (-jnp.arange(rd, dtype=F32) / rd)
    ang = jnp.broadcast_to(pos[:, :, None, None] * inv, (DEC_SEQ, 2, 2, rd)).reshape(DEC_SEQ, HEAD_DIM)
    cos, sin = jnp.cos(ang), jnp.sin(ang)
    first = (jnp.arange(HEAD_DIM) % (2 * rd)) < rd
    return cos, jnp.where(first, -sin, 0.0), jnp.where(first, 0.0, sin)


def _seq_position(chunk):
    ctx_per_seq = SEQ // CHUNK
    lat_per_seq = DEC_SEQ // CHUNK
    is_ctx = chunk < CTX_CHUNKS
    idx = jnp.where(is_ctx, chunk % ctx_per_seq, (chunk - CTX_CHUNKS) % lat_per_seq)
    per = jnp.where(is_ctx, ctx_per_seq, lat_per_seq)
    return idx == 0, idx == per - 1


def _lru_kernel(lx_ref, lg_ref, cw_ref, cb_ref, w4_ref, b4_ref, lam_ref, keep_ref, h0_ref,
                o_ref, hend_ref, a_s, b_s, tot_s, loc_s, hin_s):
    tile = pl.program_id(1)
    nch = LRU_TILE_CHUNKS
    row = lax.broadcasted_iota(jnp.int32, (CHUNK, LANES), 0)
    lam = lam_ref[...]
    c_log_sig = LRU_C * (jnp.minimum(lam, 0.0) - jnp.log(1.0 + jnp.exp(-jnp.abs(lam))))
    cw = cw_ref[...]
    cb = cb_ref[...]
    w4 = w4_ref[...]
    b4 = b4_ref[...]

    def gates(c, carry):
        r0 = pl.multiple_of(c * CHUNK, CHUNK)
        first, last = _seq_position(tile * nch + c)
        lo = pl.multiple_of(jnp.maximum(r0 - SUBLANES, 0), SUBLANES)
        hi = pl.multiple_of(jnp.minimum(r0 + CHUNK, LRU_TILE - SUBLANES), SUBLANES)
        ext = jnp.concatenate(
            [lx_ref[pl.ds(lo, SUBLANES), :], lx_ref[pl.ds(r0, CHUNK), :], lx_ref[pl.ds(hi, SUBLANES), :]],
            axis=0)
        xm2 = jnp.where(jnp.logical_and(first, row < 2), 0.0, ext[6:6 + CHUNK])
        xm1 = jnp.where(jnp.logical_and(first, row < 1), 0.0, ext[7:7 + CHUNK])
        x0 = ext[8:8 + CHUNK]
        xp1 = jnp.where(jnp.logical_and(last, row >= CHUNK - 1), 0.0, ext[9:9 + CHUNK])
        u = xm2 * cw[0:1] + xm1 * cw[1:2] + x0 * cw[2:3] + xp1 * cw[3:4] + cb
        g = _dot(u.astype(BF16), w4) + b4
        s0 = pl.multiple_of(c * CHUNK_PITCH, SUBLANES)
        for d in range(2):
            r = _sigmoid_tanh(g[:, (2 * d) * LANES:(2 * d + 1) * LANES])
            ig = _sigmoid_tanh(g[:, (2 * d + 1) * LANES:(2 * d + 2) * LANES])
            a = jnp.exp(r * c_log_sig[d:d + 1])
            a_s[d, pl.ds(s0, CHUNK), :] = a
            b_s[d, pl.ds(s0, CHUNK), :] = jnp.sqrt(1.0 - a * a) * ig * u
        return carry

    lax.fori_loop(0, nch, gates, 0, unroll=2)

    def step(k, carry):
        out = []
        for d in range(2):
            acc_a, h = carry[2 * d], carry[2 * d + 1]
            t = k if d == 0 else CHUNK - 1 - k
            idx = pl.ds(t, nch, stride=CHUNK_PITCH)
            a_t = a_s[d, idx, :]
            h = a_t * h + b_s[d, idx, :]
            acc_a = acc_a * a_t
            a_s[d, idx, :] = acc_a
            b_s[d, idx, :] = h
            out += [acc_a, h]
        return tuple(out)

    ones = jnp.ones((nch, LANES), F32)
    zeros = jnp.zeros((nch, LANES), F32)
    fin = lax.fori_loop(0, CHUNK, step, (ones, zeros, ones, zeros))
    for d in range(2):
        tot_s[d] = fin[2 * d]
        loc_s[d] = fin[2 * d + 1]

    for d in range(2):
        prev = jnp.zeros((1, LANES), F32)
        order = range(nch) if d == 0 else range(nch - 1, -1, -1)
        for c in order:
            hin = keep_ref[d, c:c + 1, :] * prev + h0_ref[d, c:c + 1, :]
            hin_s[d, c:c + 1, :] = hin
            prev = tot_s[d, c:c + 1, :] * hin + loc_s[d, c:c + 1, :]
            hend_ref[d, c:c + 1, :] = prev

    def emit(c, carry):
        r0 = pl.multiple_of(c * CHUNK, CHUNK)
        s0 = pl.multiple_of(c * CHUNK_PITCH, SUBLANES)
        y = None
        for d in range(2):
            h = b_s[d, pl.ds(s0, CHUNK), :] + a_s[d, pl.ds(s0, CHUNK), :] * hin_s[d, pl.ds(c, 1), :]
            y = h if y is None else y + h
        o_ref[pl.ds(r0, CHUNK), :] = (y * _gelu_tanh(lg_ref[pl.ds(r0, CHUNK), :])).astype(o_ref.dtype)
        return carry

    lax.fori_loop(0, nch, emit, 0, unroll=2)


def _lru(p, conv_w, conv_b, w4, b4, lam, keep, h0):
    nt = N_TOK // LRU_TILE
    nch = LRU_TILE_CHUNKS
    lx_blk = COL_LX // LRU_BLK
    lg_blk = COL_LG // LRU_BLK
    return pl.pallas_call(
        _lru_kernel,
        grid=(LRU_BLOCKS, nt),
        in_specs=[
            pl.BlockSpec((LRU_TILE, LRU_BLK), lambda n, r: (r, lx_blk + n)),
            pl.BlockSpec((LRU_TILE, LRU_BLK), lambda n, r: (r, lg_blk + n)),
            pl.BlockSpec((CONV_W, LRU_BLK), lambda n, r: (0, n)),
            pl.BlockSpec((1, LRU_BLK), lambda n, r: (0, n)),
            pl.BlockSpec((None, LRU_BLK, 4 * LRU_BLK), lambda n, r: (n, 0, 0)),
            pl.BlockSpec((None, 1, 4 * LRU_BLK), lambda n, r: (n, 0, 0)),
            pl.BlockSpec((2, LRU_BLK), lambda n, r: (0, n)),
            pl.BlockSpec((2, nch, LRU_BLK), lambda n, r: (0, r, 0)),
            pl.BlockSpec((2, nch, LRU_BLK), lambda n, r: (0, r, n)),
        ],
        out_specs=[
            pl.BlockSpec((LRU_TILE, LRU_BLK), lambda n, r: (r, n)),
            pl.BlockSpec((2, nch, LRU_BLK), lambda n, r: (0, r, n)),
        ],
        out_shape=[
            jax.ShapeDtypeStruct((N_TOK, D_LRU), BF16),
            jax.ShapeDtypeStruct((2, N_CHUNKS, D_LRU), F32),
        ],
        scratch_shapes=[
            pltpu.VMEM((2, nch * CHUNK_PITCH, LANES), F32),
            pltpu.VMEM((2, nch * CHUNK_PITCH, LANES), F32),
            pltpu.VMEM((2, nch, LANES), F32),
            pltpu.VMEM((2, nch, LANES), F32),
            pltpu.VMEM((2, nch, LANES), F32),
        ],
        compiler_params=_params("arbitrary", "arbitrary"),
        name="rglru",
    )(p, p, conv_w, conv_b, w4, b4, lam, keep, h0)


def _lru_chain_inputs(state_l):
    c = jnp.arange(N_CHUNKS)
    ctx_per_seq = SEQ // CHUNK
    lat_per_seq = DEC_SEQ // CHUNK
    is_ctx = c < CTX_CHUNKS
    idx = jnp.where(is_ctx, c % ctx_per_seq, (c - CTX_CHUNKS) % lat_per_seq)
    per = jnp.where(is_ctx, ctx_per_seq, lat_per_seq)
    start = jnp.stack([idx == 0, idx == per - 1])
    keep = jnp.broadcast_to(jnp.where(start, 0.0, 1.0)[:, :, None], (2, N_CHUNKS, LANES)).astype(F32)
    lat_b = jnp.clip((c - CTX_CHUNKS) // lat_per_seq, 0, DEC_BATCH - 1)
    seed = jnp.transpose(state_l, (1, 0, 2))[:, lat_b, :]
    h0 = jnp.where((start & ~is_ctx[None, :])[:, :, None], seed, 0.0).astype(F32)
    return keep, h0


def _pool_kernel(x_ref, w_ref, sc_ref, o_ref):
    tile = pl.program_id(0)
    ext_rows = POOL_CHUNK + 2 * POOL_HALO
    is_ctx = tile * POOL_TILE < N_CTX
    erow = lax.broadcasted_iota(jnp.int32, (ext_rows, LANES), 0)
    row = lax.broadcasted_iota(jnp.int32, (POOL_CHUNK, LANES), 0)

    def shift_up(x, k):
        return x if k == 0 else pltpu.roll(x, ext_rows - k, 0)

    for g, win in enumerate(POOL_WINDOWS):
        half = win // 2
        cols = slice(g * POOL_GROUP, (g + 1) * POOL_GROUP)
        w = w_ref[g]
        sc = sc_ref[:, cols]

        def body(c, carry):
            r0 = pl.multiple_of(c * POOL_CHUNK, POOL_CHUNK)
            seq_len = jnp.where(is_ctx, SEQ, DEC_SEQ)
            t0 = jnp.where(is_ctx, 0, (tile * POOL_TILE + r0) & (DEC_SEQ - 1))
            first = t0 == 0
            last = t0 + POOL_CHUNK == seq_len
            lo = pl.multiple_of(jnp.maximum(r0 - POOL_HALO, 0), POOL_HALO)
            hi = pl.multiple_of(jnp.minimum(r0 + POOL_CHUNK, POOL_TILE - POOL_HALO), POOL_HALO)
            cur = x_ref[pl.ds(r0, POOL_CHUNK), cols]
            ext = jnp.concatenate(
                [x_ref[pl.ds(lo, POOL_HALO), cols], cur, x_ref[pl.ds(hi, POOL_HALO), cols]], axis=0)
            outside = jnp.logical_or(jnp.logical_and(first, erow < POOL_HALO),
                                     jnp.logical_and(last, erow >= POOL_HALO + POOL_CHUNK))
            acc = jnp.where(outside, 0.0, ext)
            span = 1
            while span < win:
                acc = acc + shift_up(acc, span)
                span *= 2
            total = shift_up(acc, POOL_HALO - half)[:POOL_CHUNK]
            t = t0 + row
            cnt = jnp.minimum(t + half, seq_len) - jnp.maximum(t - half, 0)
            pooled = total / cnt.astype(F32) - cur
            y = _dot(pooled.astype(BF16), w) * sc
            o_ref[pl.ds(r0, POOL_CHUNK), cols] = y.astype(o_ref.dtype)
            return carry

        lax.fori_loop(0, POOL_TILE // POOL_CHUNK, body, 0, unroll=2)


def _pool(p, pool_w_bf16, pool_scale):
    blk = COL_PU // D_POOL
    return pl.pallas_call(
        _pool_kernel,
        grid=(N_TOK // POOL_TILE,),
        in_specs=[
            pl.BlockSpec((POOL_TILE, D_POOL), lambda i: (i, blk)),
            pl.BlockSpec((len(POOL_WINDOWS), POOL_GROUP, POOL_GROUP), lambda i: (0, 0, 0)),
            pl.BlockSpec((1, D_POOL), lambda i: (0, 0)),
        ],
        out_specs=pl.BlockSpec((POOL_TILE, D_POOL), lambda i: (i, 0)),
        out_shape=jax.ShapeDtypeStruct((N_TOK, D_POOL), BF16),
        compiler_params=_params("arbitrary"),
        name="pool_mixer",
    )(p, pool_w_bf16, pool_scale)


OUTPROJ_TM = 512


def _outproj_kernel(attn_ref, lru_ref, pool_ref, x_ref, gate_ref, wa_ref, wl_ref, wp_ref, o_ref):
    mix = _dot(attn_ref[...], wa_ref[...]) + _dot(lru_ref[...], wl_ref[...]) + _dot(pool_ref[...], wp_ref[...])
    o_ref[...] = x_ref[...] + gate_ref[...] * mix


def _outproj(attn, lru, pool, x, mod, w_out_bf16):
    tm = OUTPROJ_TM
    return pl.pallas_call(
        _outproj_kernel,
        grid=(N_TOK // tm,),
        in_specs=[
            pl.BlockSpec((tm, D_ATTN), lambda i: (i, 0)),
            pl.BlockSpec((tm, D_LRU), lambda i: (i, 0)),
            pl.BlockSpec((tm, D_POOL), lambda i: (i, 0)),
            pl.BlockSpec((tm, D_MODEL), lambda i: (i, 0)),
            pl.BlockSpec((None, 1, D_MODEL), lambda i: (_group_of_tile(i, tm) * N_MOD + 2, 0, 0)),
            pl.BlockSpec((D_ATTN, D_MODEL), lambda i: (0, 0)),
            pl.BlockSpec((D_LRU, D_MODEL), lambda i: (D_ATTN // D_LRU, 0)),
            pl.BlockSpec((D_POOL, D_MODEL), lambda i: ((D_ATTN + D_LRU) // D_POOL, 0)),
        ],
        out_specs=pl.BlockSpec((tm, D_MODEL), lambda i: (i, 0)),
        out_shape=jax.ShapeDtypeStruct((N_TOK, D_MODEL), F32),
        compiler_params=_params("arbitrary"),
        name="outproj",
    )(attn, lru, pool, x, mod, w_out_bf16, w_out_bf16, w_out_bf16)


FFN_TM = 1024
FFN_TF = 512
FFN_TN = 512
FFN_CAST_COLS = 2 * FFN_TF


def _ffn_kernel(x_ref, g_ref, shift_ref, scale_ref, gate_ref, wg_ref, wu_ref, w2_ref, fg_ref, *rest,
                final_norm, cast_next):
    if cast_next:
        w1n_f32_ref, w2n_f32_ref, o_ref, w1n_ref, w2n_ref, h_ref = rest
    else:
        o_ref, h_ref = rest
    i, j = pl.program_id(0), pl.program_id(1)

    @pl.when(j == 0)
    def _():
        _norm_mod_store(h_ref, x_ref, g_ref[...], scale_ref[...], shift_ref[...], FFN_TM, copy_ref=o_ref)

    h = h_ref[...]
    act = (_silu(_dot(h, wg_ref[...])) * _dot(h, wu_ref[...])).astype(BF16)
    for n in range(D_MODEL // FFN_TN):
        cols = slice(n * FFN_TN, (n + 1) * FFN_TN)
        o_ref[:, cols] += gate_ref[:, cols] * _dot(act, w2_ref[:, cols])

    if cast_next:
        @pl.when(i < CAST_TILES)
        def _():
            w1n_ref[...] = w1n_f32_ref[...].astype(BF16)
            w2n_ref[...] = w2n_f32_ref[...].astype(BF16)

    if final_norm:
        @pl.when(j == pl.num_programs(1) - 1)
        def _():
            gain = fg_ref[...]

            def body(r, carry):
                r0 = pl.multiple_of(r * NORM_ROWS, NORM_ROWS)
                x = o_ref[pl.ds(r0, NORM_ROWS), :]
                ms = jnp.mean(x * x, axis=-1, keepdims=True)
                o_ref[pl.ds(r0, NORM_ROWS), :] = x * lax.rsqrt(ms + RMS_EPS) * gain
                return carry

            lax.fori_loop(0, FFN_TM // NORM_ROWS, body, 0, unroll=NORM_UNROLL)


def _ffn(x, norm_g, mod, w1_l, w2_l, final_g, layer, tile0, ntiles, final_norm, w1_f32=None, w2_f32=None):
    tm, tf = FFN_TM, FFN_TF
    nf = D_FF // tf
    cast_next = w1_f32 is not None

    def mod_spec(which):
        return pl.BlockSpec((None, 1, D_MODEL),
                            lambda i, j: (_group_of_tile(tile0 + i, tm) * N_MOD + which, 0, 0))

    in_specs = [
        pl.BlockSpec((tm, D_MODEL), lambda i, j: (tile0 + i, 0), pipeline_mode=pl.Buffered(1)),
        pl.BlockSpec((1, D_MODEL), lambda i, j: (0, 0)),
        mod_spec(3),
        mod_spec(4),
        mod_spec(5),
        pl.BlockSpec((D_MODEL, tf), lambda i, j: (0, j)),
        pl.BlockSpec((D_MODEL, tf), lambda i, j: (0, nf + j)),
        pl.BlockSpec((tf, D_MODEL), lambda i, j: (j, 0)),
        pl.BlockSpec((1, D_MODEL), lambda i, j: (0, 0)),
    ]
    out_specs = [pl.BlockSpec((tm, D_MODEL), lambda i, j: (i, 0))]
    out_shape = [jax.ShapeDtypeStruct((ntiles * tm, D_MODEL), F32)]
    args = [x, norm_g, mod, mod, mod, w1_l, w1_l, w2_l, final_g]
    if cast_next:
        assert ntiles >= CAST_TILES and 2 * D_FF == nf * FFN_CAST_COLS
        nxt = layer + 1
        w1_idx = lambda i, j: _cast_idx(i, j, nf - 1)
        w2_idx = lambda i, j: _cast_idx(i, j, nf - 1)[::-1]
        in_specs += [
            pl.BlockSpec((None, CAST_ROWS, FFN_CAST_COLS), lambda i, j: (nxt,) + w1_idx(i, j)),
            pl.BlockSpec((None, tf, CAST_ROWS), lambda i, j: (nxt,) + w2_idx(i, j)),
        ]
        out_specs += [pl.BlockSpec((CAST_ROWS, FFN_CAST_COLS), w1_idx), pl.BlockSpec((tf, CAST_ROWS), w2_idx)]
        out_shape += [jax.ShapeDtypeStruct((D_MODEL, 2 * D_FF), BF16), jax.ShapeDtypeStruct((D_FF, D_MODEL), BF16)]
        args += [w1_f32, w2_f32]
    out = pl.pallas_call(
        functools.partial(_ffn_kernel, final_norm=final_norm, cast_next=cast_next),
        grid=(ntiles, nf),
        in_specs=in_specs,
        out_specs=out_specs,
        out_shape=out_shape,
        scratch_shapes=[pltpu.VMEM((tm, D_MODEL), BF16)],
        compiler_params=_params("arbitrary", "arbitrary"),
        name="ffn_final" if final_norm else ("ffn_cast" if cast_next else "ffn"),
    )(*args)
    return out if cast_next else out[0]


def kernel(x_prompt, x_sample, cache_k, cache_v, state_lru, c, c_ctx, mod_w, mod_b, norm_mix, norm_ffn,
           w_in, attn_sink, conv_w, conv_b, lru_wa, lru_ba, lru_wx, lru_bx, lru_lambda, pool_w,
           pool_scale, w_out, ffn_w1, ffn_w2, norm_final):
    x = jnp.concatenate([x_prompt.reshape(N_CTX, D_MODEL), x_sample.reshape(N_LAT, D_MODEL)], axis=0)

    cond = jnp.zeros((COND_ROWS, D_MODEL), F32).at[0].set(c_ctx).at[1:1 + DEC_BATCH].set(c)
    mod_all = _mod_vectors(cond, mod_w, mod_b)
    mod_all = mod_all[:, :N_GROUPS].reshape(DEPTH, N_GROUPS * N_MOD, 1, D_MODEL)

    w_in_l = w_in[0].astype(BF16)
    w_out_l = w_out[0].astype(BF16)
    w1_l = ffn_w1[0].astype(BF16)
    w2_l = ffn_w2[0].astype(BF16)
    pool_w_b = pool_w.astype(BF16)
    w4 = jnp.concatenate([lru_wa[:, 0], lru_wx[:, 0], lru_wa[:, 1], lru_wx[:, 1]], axis=-1).astype(BF16)
    ba = lru_ba.reshape(DEPTH, 2, LRU_BLOCKS, LRU_BLK)
    bx = lru_bx.reshape(DEPTH, 2, LRU_BLOCKS, LRU_BLK)
    b4 = jnp.stack([ba[:, 0], bx[:, 0], ba[:, 1], bx[:, 1]], axis=2).reshape(DEPTH, LRU_BLOCKS, 1, 4 * LRU_BLK)
    cache_k2 = cache_k.reshape(DEC_BATCH, DEPTH, PAST_LEN, D_KV)
    cache_v2 = cache_v.reshape(DEC_BATCH, DEPTH, PAST_LEN, D_KV)
    rope_tabs = _rope_tables()

    kc = jnp.zeros((BATCH, DEPTH, SEQ, D_KV), F32)
    vc = jnp.zeros((BATCH, DEPTH, SEQ, D_KV), F32)
    attn = jnp.zeros((N_TOK, D_ATTN), BF16)
    hs_new = []
    for l in range(DEPTH):
        mod = mod_all[l]
        last = l == DEPTH - 1
        w_out_cur = w_out_l
        if last:
            p, kc, vc = _inproj(x, norm_mix[l][None, :], mod, w_in_l, kc, vc, l)
        else:
            p, kc, vc, w_in_l, w_out_l = _inproj(x, norm_mix[l][None, :], mod, w_in_l, kc, vc, l, w_in, w_out)

        attn = _ctx_attention(p, attn_sink[l], attn)
        attn = _lat_attention(p, attn_sink[l], cache_k2, cache_v2, rope_tabs, attn, l)
        keep, h0 = _lru_chain_inputs(state_lru[:, l])
        lru, hend = _lru(p, conv_w[l], conv_b[l][None, :], w4[l], b4[l], lru_lambda[l], keep, h0)
        pool = _pool(p, pool_w_b[l], pool_scale[l][None, :])

        ctx_per_seq = SEQ // CHUNK
        fwd_fin = hend[0, ctx_per_seq - 1:CTX_CHUNKS:ctx_per_seq]
        bwd_fin = hend[1, 0:CTX_CHUNKS:ctx_per_seq]
        hs_new.append(jnp.stack([fwd_fin, bwd_fin], axis=1))

        x = _outproj(attn, lru, pool, x, mod, w_out_cur)
        ffn_args = (x, norm_ffn[l][None, :], mod, w1_l, w2_l, norm_final[None, :], l)
        if not last:
            x, w1_l, w2_l = _ffn(*ffn_args, 0, N_TOK // FFN_TM, False, ffn_w1, ffn_w2)

    y_prompt = _ffn(*ffn_args, 0, N_CTX // FFN_TM, True).reshape(BATCH, SEQ, D_MODEL)
    y_sample = _ffn(*ffn_args, N_CTX // FFN_TM, N_LAT // FFN_TM, True).reshape(DEC_BATCH, DEC_SEQ, D_MODEL)
    cache_shape = (BATCH, DEPTH, SEQ, N_KV_HEADS, HEAD_DIM)
    return (y_prompt, y_sample, kc.reshape(cache_shape), vc.reshape(cache_shape), jnp.stack(hs_new, axis=1))
```

```python
import functools
import math

import jax
import jax.numpy as jnp
from jax import lax
from jax.experimental import pallas as pl
from jax.experimental.pallas import tpu as pltpu

F32 = jnp.float32
BF16 = jnp.bfloat16

D_MODEL = 2048
BATCH = 32
SEQ = 256
DEPTH = 4
DEC_BATCH = 2
DEC_SEQ = 2048
PAST_LEN = 512
GRID_W = 64
HEAD_DIM = 128
N_HEADS = 8
N_KV_HEADS = 2
KV_GROUP = N_HEADS // N_KV_HEADS
D_ATTN = N_HEADS * HEAD_DIM
D_KV = N_KV_HEADS * HEAD_DIM
WINDOW = 128
Q_BLOCK = 128
ROPE_THETA = 10000.0
D_LRU = D_MODEL // 4
LRU_BLOCKS = 4
LRU_BLK = D_LRU // LRU_BLOCKS
CONV_W = 4
LRU_C = 8.0
D_POOL = D_MODEL // 4
POOL_WINDOWS = (2, 4, 8, 16)
POOL_GROUP = D_POOL // len(POOL_WINDOWS)
D_IN = D_ATTN + 2 * D_KV + 2 * D_LRU + D_POOL
D_FF = 5632
N_MOD = 6
RMS_EPS = 1e-6
NEG_INF = -1e30

N_CTX = BATCH * SEQ
N_LAT = DEC_BATCH * DEC_SEQ
N_TOK = N_CTX + N_LAT
N_GROUPS = 1 + DEC_BATCH
COND_ROWS = 16

COL_Q = 0
COL_K = D_ATTN
COL_V = D_ATTN + D_KV
COL_LX = D_ATTN + 2 * D_KV
COL_LG = COL_LX + D_LRU
COL_PU = COL_LG + D_LRU

VMEM_LIMIT = 56 * 1024 * 1024
MATMUL_STAGING_BYTES = 6 * 1024 * 1024

NORM_ROWS = 32
NORM_UNROLL = 4
LANES = 128
SUBLANES = 8

CHUNK = 128
CHUNK_PITCH = 136
SQRT_TINY = 1e-30
SCAN_UNROLL = 4
LRU_TILE_CHUNKS = 48
LRU_TILE = LRU_TILE_CHUNKS * CHUNK
N_CHUNKS = N_TOK // CHUNK
CTX_CHUNKS = N_CTX // CHUNK

POOL_TILE = DEC_SEQ
POOL_CHUNK = 256
POOL_HALO = 8


def _dot(a, b):
    return jnp.dot(a, b, preferred_element_type=F32)


def _dot_nt(a, b):
    return lax.dot_general(a, b, (((1,), (1,)), ((), ())), preferred_element_type=F32)


def _sigmoid(x):
    return 1.0 / (1.0 + jnp.exp(-x))


def _sigmoid_tanh(x):
    return 0.5 * jnp.tanh(0.5 * x) + 0.5


def _silu(x):
    return x * _sigmoid(x)


def _gelu_tanh(x):
    c = math.sqrt(2.0 / math.pi)
    return 0.5 * x * (1.0 + jnp.tanh(c * (x + 0.044715 * (x * x * x))))


def _params(*sem, vmem_limit=VMEM_LIMIT):
    return pltpu.CompilerParams(dimension_semantics=sem, vmem_limit_bytes=vmem_limit)


CAST_TILES = 8
CAST_ROWS = D_MODEL // CAST_TILES


def _cast_idx(i, j, j_last):
    active = i < CAST_TILES
    return jnp.minimum(i, CAST_TILES - 1), jnp.where(active, jnp.minimum(j, j_last), j_last)


def _group_of_tile(i, tm):
    n_ctx_tiles = N_CTX // tm
    tiles_per_lat = DEC_SEQ // tm
    return jnp.where(i < n_ctx_tiles, 0, 1 + (i - n_ctx_tiles) // tiles_per_lat)


def _norm_mod_store(h_ref, x_ref, g, scale, shift, rows, copy_ref=None):
    gain = g * (1.0 + scale)

    def body(r, carry):
        r0 = pl.multiple_of(r * NORM_ROWS, NORM_ROWS)
        x = x_ref[pl.ds(r0, NORM_ROWS), :]
        ms = jnp.mean(x * x, axis=-1, keepdims=True)
        h_ref[pl.ds(r0, NORM_ROWS), :] = (x * lax.rsqrt(ms + RMS_EPS) * gain + shift).astype(h_ref.dtype)
        if copy_ref is not None:
            copy_ref[pl.ds(r0, NORM_ROWS), :] = x
        return carry

    lax.fori_loop(0, rows // NORM_ROWS, body, 0, unroll=NORM_UNROLL)


MOD_TN = 1024


def _mod_kernel(cond_ref, w_ref, b_ref, o_ref):
    s = _silu(cond_ref[...]).astype(BF16)
    o_ref[...] = _dot(s, w_ref[...].astype(BF16)) + b_ref[...]


def _mod_vectors(cond, mod_w, mod_b):
    n = N_MOD * D_MODEL
    return pl.pallas_call(
        _mod_kernel,
        grid=(DEPTH, n // MOD_TN),
        in_specs=[
            pl.BlockSpec((COND_ROWS, D_MODEL), lambda l, j: (0, 0)),
            pl.BlockSpec((None, D_MODEL, MOD_TN), lambda l, j: (l, 0, j)),
            pl.BlockSpec((None, 1, MOD_TN), lambda l, j: (l, 0, j)),
        ],
        out_specs=pl.BlockSpec((None, COND_ROWS, MOD_TN), lambda l, j: (l, 0, j)),
        out_shape=jax.ShapeDtypeStruct((DEPTH, COND_ROWS, n), F32),
        compiler_params=_params("arbitrary", "arbitrary"),
        name="mod_vectors",
    )(cond, mod_w, mod_b.reshape(DEPTH, 1, n))


INPROJ_TM = 512
INPROJ_TILES = N_TOK // INPROJ_TM
INPROJ_CAST_COLS = 1024


def _inproj_kernel(*refs, cast_next, two_x):
    refs = list(refs)
    xa_ref = refs.pop(0)
    xb_ref = refs.pop(0) if two_x else None
    g_ref, shift_ref, scale_ref, w_ref, kc_in_ref, vc_in_ref = refs[:6]
    del kc_in_ref, vc_in_ref
    refs = refs[6:]
    if cast_next:
        win_f32_ref, wout_f32_ref, o_ref, kc_ref, vc_ref, win_ref, wout_ref, h0_ref, h1_ref = refs
    else:
        o_ref, kc_ref, vc_ref, h0_ref, h1_ref = refs
    s = pl.program_id(0)
    odd = s % 2 == 1

    @pl.when(s == 0)
    def _():
        h1_ref[...] = jnp.zeros_like(h1_ref)

    gain = g_ref[...] * (1.0 + scale_ref[...])
    shift = shift_ref[...]
    from_a = s < N_CTX // INPROJ_TM

    def step(h_write, h_read):
        for c in range(INPROJ_TM // NORM_ROWS):
            rows = slice(c * NORM_ROWS, (c + 1) * NORM_ROWS)
            x = xa_ref[rows, :]
            if two_x:
                x = jnp.where(from_a, x, xb_ref[rows, :])
            ms = jnp.mean(x * x, axis=-1, keepdims=True)
            h_write[rows, :] = (x * lax.rsqrt(ms + RMS_EPS) * gain + shift).astype(BF16)
        o_ref[...] = _dot(h_read[...], w_ref[...])

    @pl.when(jnp.logical_not(odd))
    def _():
        step(h0_ref, h1_ref)

    @pl.when(odd)
    def _():
        step(h1_ref, h0_ref)

    @pl.when(jnp.logical_and(s >= 1, s <= N_CTX // INPROJ_TM))
    def _():
        kc_ref[...] = o_ref[:, COL_K:COL_K + D_KV].reshape(kc_ref.shape)
        vc_ref[...] = o_ref[:, COL_V:COL_V + D_KV].reshape(vc_ref.shape)

    if cast_next:
        @pl.when(s < CAST_TILES * (D_IN // INPROJ_CAST_COLS))
        def _():
            win_ref[...] = win_f32_ref[...].astype(BF16)

        @pl.when(s < CAST_TILES * (D_MODEL // INPROJ_CAST_COLS))
        def _():
            wout_ref[...] = wout_f32_ref[...].astype(BF16)


def _inproj(xs, norm_g, mod, w_in_l, kc, vc, layer, w_in_f32=None, w_out_f32=None):
    tm = INPROJ_TM
    nt = INPROJ_TILES
    cast_next = w_in_f32 is not None
    two_x = len(xs) == 2
    seqs = tm // SEQ
    n_ctx_tiles = N_CTX // tm

    def norm_tile(s):
        return jnp.minimum(s, nt - 1)

    def mm_tile(s):
        return jnp.maximum(s - 1, 0)

    def mod_spec(which):
        return pl.BlockSpec((None, 1, D_MODEL), lambda s: (_group_of_tile(norm_tile(s), tm) * N_MOD + which, 0, 0))

    def cast_idx(ncols):
        def idx(s):
            t = jnp.minimum(s, CAST_TILES * ncols - 1)
            return t // ncols, t % ncols
        return idx

    cache_spec = pl.BlockSpec((seqs, None, SEQ, D_KV),
                              lambda s: (jnp.minimum(mm_tile(s), n_ctx_tiles - 1), layer, 0, 0))
    cache_shape = jax.ShapeDtypeStruct((BATCH, DEPTH, SEQ, D_KV), F32)
    if two_x:
        x_specs = [pl.BlockSpec((tm, D_MODEL), lambda s: (jnp.minimum(s, n_ctx_tiles - 1), 0)),
                   pl.BlockSpec((tm, D_MODEL), lambda s: (jnp.clip(s - n_ctx_tiles, 0, nt - n_ctx_tiles - 1), 0))]
    else:
        x_specs = [pl.BlockSpec((tm, D_MODEL), lambda s: (norm_tile(s), 0))]
    in_specs = x_specs + [
        pl.BlockSpec((1, D_MODEL), lambda s: (0, 0)),
        mod_spec(0),
        mod_spec(1),
        pl.BlockSpec((D_MODEL, D_IN), lambda s: (0, 0), pipeline_mode=pl.Buffered(1)),
        pl.BlockSpec(memory_space=pl.ANY),
        pl.BlockSpec(memory_space=pl.ANY),
    ]
    out_specs = [pl.BlockSpec((tm, D_IN), lambda s: (mm_tile(s), 0)), cache_spec, cache_spec]
    out_shape = [jax.ShapeDtypeStruct((N_TOK, D_IN), F32), cache_shape, cache_shape]
    args = list(xs) + [norm_g, mod, mod, w_in_l, kc, vc]
    n_in = len(args)
    if cast_next:
        nxt = layer + 1
        cc = INPROJ_CAST_COLS
        win_idx = cast_idx(D_IN // cc)
        wout_idx = cast_idx(D_MODEL // cc)
        in_specs += [
            pl.BlockSpec((None, CAST_ROWS, cc), lambda s: (nxt,) + win_idx(s)),
            pl.BlockSpec((None, CAST_ROWS, cc), lambda s: (nxt,) + wout_idx(s)),
        ]
        out_specs += [pl.BlockSpec((CAST_ROWS, cc), win_idx), pl.BlockSpec((CAST_ROWS, cc), wout_idx)]
        out_shape += [jax.ShapeDtypeStruct((D_MODEL, D_IN), BF16), jax.ShapeDtypeStruct((D_MODEL, D_MODEL), BF16)]
        args += [w_in_f32, w_out_f32]
    cast_bytes = 2 * 2 * CAST_ROWS * INPROJ_CAST_COLS * (4 + 2) if cast_next else 0
    vmem_bytes = (D_MODEL * D_IN * 2 + 2 * len(xs) * tm * D_MODEL * 4 + 2 * tm * D_IN * 4 + 2 * tm * D_MODEL * 2
                  + 2 * 2 * tm * D_KV * 4 + cast_bytes + MATMUL_STAGING_BYTES)
    return pl.pallas_call(
        functools.partial(_inproj_kernel, cast_next=cast_next, two_x=two_x),
        grid=(nt + 1,),
        in_specs=in_specs,
        out_specs=out_specs,
        out_shape=out_shape,
        input_output_aliases={n_in - 2: 1, n_in - 1: 2},
        scratch_shapes=[pltpu.VMEM((tm, D_MODEL), BF16), pltpu.VMEM((tm, D_MODEL), BF16)],
        compiler_params=_params("arbitrary", vmem_limit=vmem_bytes),
        name="inproj_cast" if cast_next else "inproj",
    )(*args)


def _sink_column(sink_ref, kv_head, rows_per_head):
    n = KV_GROUP * rows_per_head
    row = lax.broadcasted_iota(jnp.int32, (n, 1), 0)
    col = jnp.full((n, 1), sink_ref[kv_head * KV_GROUP + KV_GROUP - 1], F32)
    for hh in range(KV_GROUP - 2, -1, -1):
        col = jnp.where(row < (hh + 1) * rows_per_head, sink_ref[kv_head * KV_GROUP + hh], col)
    return col


def _softmax_pv(scores, values, sink_col):
    m = sink_col
    for s in scores:
        m = jnp.maximum(m, jnp.max(s, axis=-1, keepdims=True))
    denom = jnp.exp(sink_col - m)
    out = None
    for s, v in zip(scores, values):
        p = jnp.exp(s - m)
        denom = denom + jnp.sum(p, axis=-1, keepdims=True)
        pv = _dot(p.astype(BF16), v)
        out = pv if out is None else out + pv
    return out / denom


def _stack_heads(q, kv_head):
    return jnp.concatenate(
        [q[:, (kv_head * KV_GROUP + hh) * HEAD_DIM:(kv_head * KV_GROUP + hh + 1) * HEAD_DIM]
         for hh in range(KV_GROUP)], axis=0)


def _ctx_attn_kernel(sink_ref, q_ref, kv_ref, buf_hbm_ref, o_ref):
    del buf_hbm_ref
    scale = HEAD_DIM ** -0.5
    q = q_ref[...]
    kv = kv_ref[...]
    for g in range(N_KV_HEADS):
        k = kv[:, g * HEAD_DIM:(g + 1) * HEAD_DIM].astype(BF16)
        v = kv[:, D_KV + g * HEAD_DIM:D_KV + (g + 1) * HEAD_DIM].astype(BF16)
        qs = _stack_heads(q, g).astype(BF16)
        s = _dot_nt(qs, k) * scale
        o = _softmax_pv([s], [v], _sink_column(sink_ref, g, SEQ))
        for hh in range(KV_GROUP):
            h = g * KV_GROUP + hh
            o_ref[:, h * HEAD_DIM:(h + 1) * HEAD_DIM] = o[hh * SEQ:(hh + 1) * SEQ].astype(o_ref.dtype)


def _ctx_attention(p, sink, buf):
    kv_blk = 2 * D_KV
    return pl.pallas_call(
        _ctx_attn_kernel,
        grid_spec=pltpu.PrefetchScalarGridSpec(
            num_scalar_prefetch=1,
            grid=(BATCH,),
            in_specs=[
                pl.BlockSpec((SEQ, D_ATTN), lambda b, s: (b, 0)),
                pl.BlockSpec((SEQ, kv_blk), lambda b, s: (b, COL_K // kv_blk)),
                pl.BlockSpec(memory_space=pl.ANY),
            ],
            out_specs=pl.BlockSpec((SEQ, D_ATTN), lambda b, s: (b, 0)),
        ),
        out_shape=jax.ShapeDtypeStruct((N_TOK, D_ATTN), BF16),
        input_output_aliases={3: 0},
        compiler_params=_params("arbitrary"),
        name="ctx_attention",
    )(sink, p, p, buf)


def _rope(x, cos, sin_lo, sin_hi):
    return x * cos + pltpu.roll(x, 96, 1) * sin_lo + pltpu.roll(x, 32, 1) * sin_hi


def _lat_attn_kernel(sink_ref, q_ref, kv_ref, ck_ref, cv_ref, cos_ref, slo_ref, shi_ref, attn_hbm_ref,
                     o_ref, kt_s, v_s, ckt_s, cv_s):
    del attn_hbm_ref
    i = pl.program_id(1)
    scale = HEAD_DIM ** -0.5
    band = Q_BLOCK + 2 * WINDOW
    band_blocks = band // Q_BLOCK

    @pl.when(i == 0)
    def _():
        for g in range(N_KV_HEADS):
            sl = slice(g * HEAD_DIM, (g + 1) * HEAD_DIM)
            k = _rope(kv_ref[:, sl], cos_ref[...], slo_ref[...], shi_ref[...])
            for t in range(DEC_SEQ // Q_BLOCK):
                kt_s[g, t] = k[t * Q_BLOCK:(t + 1) * Q_BLOCK].T.astype(BF16)
            ckt_s[g] = ck_ref[:, sl].T.astype(BF16)
        v_s[...] = kv_ref[:, D_KV:2 * D_KV].astype(BF16)
        cv_s[...] = cv_ref[...].astype(BF16)

    q0 = pl.multiple_of(i * Q_BLOCK, Q_BLOCK)
    kb0 = jnp.clip(i - WINDOW // Q_BLOCK, 0, DEC_SEQ // Q_BLOCK - band_blocks)
    k0 = pl.multiple_of(kb0 * Q_BLOCK, Q_BLOCK)
    cos = cos_ref[pl.ds(q0, Q_BLOCK), :] * scale
    slo = slo_ref[pl.ds(q0, Q_BLOCK), :] * scale
    shi = shi_ref[pl.ds(q0, Q_BLOCK), :] * scale
    cos4 = jnp.concatenate([cos] * KV_GROUP, axis=0)
    slo4 = jnp.concatenate([slo] * KV_GROUP, axis=0)
    shi4 = jnp.concatenate([shi] * KV_GROUP, axis=0)

    qpos = q0 + lax.broadcasted_iota(jnp.int32, (Q_BLOCK, band), 0)
    kpos = k0 + lax.broadcasted_iota(jnp.int32, (Q_BLOCK, band), 1)
    bias1 = jnp.where(jnp.abs(qpos - kpos) <= WINDOW, 0.0, NEG_INF).astype(F32)
    bias = jnp.concatenate([bias1] * KV_GROUP, axis=0)
    ones = jnp.ones((band + PAST_LEN, HEAD_DIM), BF16)

    q = q_ref[...]
    for g in range(N_KV_HEADS):
        sl = slice(g * HEAD_DIM, (g + 1) * HEAD_DIM)
        qs = _rope(_stack_heads(q, g), cos4, slo4, shi4).astype(BF16)
        kt = jnp.concatenate([kt_s[g, kb0 + t] for t in range(band_blocks)] + [ckt_s[g]], axis=1)
        s = _dot(qs, kt)
        s_band = s[:, :band] + bias
        s_ctx = s[:, band:]
        sink_col = _sink_column(sink_ref, g, Q_BLOCK)
        m = jnp.maximum(jnp.maximum(jnp.max(s_band, axis=-1, keepdims=True),
                                    jnp.max(s_ctx, axis=-1, keepdims=True)), sink_col)
        p = jnp.concatenate([jnp.exp(s_band - m), jnp.exp(s_ctx - m)], axis=1).astype(BF16)
        v = jnp.concatenate([v_s[pl.ds(k0, band), sl], cv_s[:, sl]], axis=0)
        pv = _dot(p, jnp.concatenate([v, ones], axis=1))
        o = pv[:, :HEAD_DIM] / (pv[:, HEAD_DIM:] + jnp.exp(sink_col - m))
        for hh in range(KV_GROUP):
            h = g * KV_GROUP + hh
            o_ref[:, h * HEAD_DIM:(h + 1) * HEAD_DIM] = (
                o[hh * Q_BLOCK:(hh + 1) * Q_BLOCK].astype(o_ref.dtype))


def _lat_attention(p, sink, cache_k, cache_v, rope_tabs, attn, layer):
    kv_blk = 2 * D_KV
    nq = DEC_SEQ // Q_BLOCK
    ctx_rows = N_CTX // Q_BLOCK
    seq_blk = N_CTX // DEC_SEQ
    tab_spec = pl.BlockSpec((DEC_SEQ, HEAD_DIM), lambda b, i, s: (0, 0))
    cache_spec = pl.BlockSpec((None, None, PAST_LEN, D_KV), lambda b, i, s: (b, layer, 0, 0))
    return pl.pallas_call(
        _lat_attn_kernel,
        grid_spec=pltpu.PrefetchScalarGridSpec(
            num_scalar_prefetch=1,
            grid=(DEC_BATCH, nq),
            in_specs=[
                pl.BlockSpec((Q_BLOCK, D_ATTN), lambda b, i, s: (ctx_rows + b * nq + i, 0)),
                pl.BlockSpec((DEC_SEQ, kv_blk), lambda b, i, s: (seq_blk + b, COL_K // kv_blk)),
                cache_spec, cache_spec, tab_spec, tab_spec, tab_spec,
                pl.BlockSpec(memory_space=pl.ANY),
            ],
            out_specs=pl.BlockSpec((Q_BLOCK, D_ATTN), lambda b, i, s: (ctx_rows + b * nq + i, 0)),
            scratch_shapes=[
                pltpu.VMEM((N_KV_HEADS, DEC_SEQ // Q_BLOCK, HEAD_DIM, Q_BLOCK), BF16),
                pltpu.VMEM((DEC_SEQ, D_KV), BF16),
                pltpu.VMEM((N_KV_HEADS, HEAD_DIM, PAST_LEN), BF16),
                pltpu.VMEM((PAST_LEN, D_KV), BF16),
            ],
        ),
        out_shape=jax.ShapeDtypeStruct((N_TOK, D_ATTN), BF16),
        input_output_aliases={8: 0},
        compiler_params=_params("arbitrary", "arbitrary"),
        name="lat_attention",
    )(sink, p, p, cache_k, cache_v, *rope_tabs, attn)


def _rope_tables():
    t = jnp.arange(DEC_SEQ)
    pos = jnp.stack([t // GRID_W, t % GRID_W], axis=-1).astype(F32)
    rd = HEAD_DIM // 4
    inv = ROPE_THETA ** (-jnp.arange(rd, dtype=F32) / rd)
    ang = jnp.broadcast_to(pos[:, :, None, None] * inv, (DEC_SEQ, 2, 2, rd)).reshape(DEC_SEQ, HEAD_DIM)
    cos, sin = jnp.cos(ang), jnp.sin(ang)
    first = (jnp.arange(HEAD_DIM) % (2 * rd)) < rd
    return cos, jnp.where(first, -sin, 0.0), jnp.where(first, 0.0, sin)


def _seq_position(chunk):
    ctx_per_seq = SEQ // CHUNK
    lat_per_seq = DEC_SEQ // CHUNK
    is_ctx = chunk < CTX_CHUNKS
    idx = jnp.where(is_ctx, chunk % ctx_per_seq, (chunk - CTX_CHUNKS) % lat_per_seq)
    per = jnp.where(is_ctx, ctx_per_seq, lat_per_seq)
    return idx == 0, idx == per - 1


def _lru_kernel(lx_ref, lg_ref, cw_ref, cb_ref, w4_ref, b4_ref, lam_ref, keep_ref, h0_ref,
                o_ref, hend_ref, a_s, b_s, tot_s, loc_s, hin_s):
    tile = pl.program_id(1)
    nch = LRU_TILE_CHUNKS
    row = lax.broadcasted_iota(jnp.int32, (CHUNK, LANES), 0)
    lam = lam_ref[...]
    half_c = (0.5 * LRU_C) * (jnp.minimum(lam, 0.0) - jnp.log(1.0 + jnp.exp(-jnp.abs(lam))))
    cw = cw_ref[...]
    cb = cb_ref[...]
    w4 = w4_ref[...]
    b4 = b4_ref[...]

    def gates(c, carry):
        r0 = pl.multiple_of(c * CHUNK, CHUNK)
        first, last = _seq_position(tile * nch + c)
        lo = pl.multiple_of(jnp.maximum(r0 - SUBLANES, 0), SUBLANES)
        hi = pl.multiple_of(jnp.minimum(r0 + CHUNK, LRU_TILE - SUBLANES), SUBLANES)
        ext = jnp.concatenate(
            [lx_ref[pl.ds(lo, SUBLANES), :], lx_ref[pl.ds(r0, CHUNK), :], lx_ref[pl.ds(hi, SUBLANES), :]],
            axis=0)
        xm2 = jnp.where(jnp.logical_and(first, row < 2), 0.0, ext[6:6 + CHUNK])
        xm1 = jnp.where(jnp.logical_and(first, row < 1), 0.0, ext[7:7 + CHUNK])
        x0 = ext[8:8 + CHUNK]
        xp1 = jnp.where(jnp.logical_and(last, row >= CHUNK - 1), 0.0, ext[9:9 + CHUNK])
        u = xm2 * cw[0:1] + xm1 * cw[1:2] + x0 * cw[2:3] + xp1 * cw[3:4] + cb
        t = jnp.tanh(_dot(u.astype(BF16), w4) + b4)
        u_half = 0.5 * u
        s0 = pl.multiple_of(c * CHUNK_PITCH, SUBLANES)
        for d in range(2):
            t_r = t[:, (2 * d) * LANES:(2 * d + 1) * LANES]
            t_i = t[:, (2 * d + 1) * LANES:(2 * d + 2) * LANES]
            a = jnp.exp(t_r * half_c[d:d + 1] + half_c[d:d + 1])
            a_s[d, pl.ds(s0, CHUNK), :] = a
            x = 1.0 - a * a
            root = x * lax.rsqrt(jnp.maximum(x, SQRT_TINY))
            b_s[d, pl.ds(s0, CHUNK), :] = root * (t_i + 1.0) * u_half
        return carry

    lax.fori_loop(0, nch, gates, 0, unroll=2)

    def time_index(k, d):
        return pl.ds(k if d == 0 else CHUNK - 1 - k, nch, stride=CHUNK_PITCH)

    def local_scan(k, carry):
        out = []
        for d in range(2):
            a_t = a_s[d, time_index(k, d), :]
            acc_a = carry[2 * d] * a_t
            h = a_t * carry[2 * d + 1] + b_s[d, time_index(k, d), :]
            a_s[d, time_index(k, d), :] = acc_a
            b_s[d, time_index(k, d), :] = h
            out += [acc_a, h]
        return tuple(out)

    ones = jnp.ones((nch, LANES), F32)
    zeros = jnp.zeros((nch, LANES), F32)
    fin = lax.fori_loop(0, CHUNK, local_scan, (ones, zeros, ones, zeros), unroll=SCAN_UNROLL)
    for d in range(2):
        tot_s[d] = fin[2 * d]
        loc_s[d] = fin[2 * d + 1]

    for d in range(2):
        prev = jnp.zeros((1, LANES), F32)
        order = range(nch) if d == 0 else range(nch - 1, -1, -1)
        for c in order:
            hin = keep_ref[d, c:c + 1, :] * prev + h0_ref[d, c:c + 1, :]
            hin_s[d, c:c + 1, :] = hin
            prev = tot_s[d, c:c + 1, :] * hin + loc_s[d, c:c + 1, :]
            hend_ref[d, c:c + 1, :] = prev

    def emit(c, carry):
        r0 = pl.multiple_of(c * CHUNK, CHUNK)
        s0 = pl.multiple_of(c * CHUNK_PITCH, SUBLANES)
        y = None
        for d in range(2):
            h = b_s[d, pl.ds(s0, CHUNK), :] + a_s[d, pl.ds(s0, CHUNK), :] * hin_s[d, pl.ds(c, 1), :]
            y = h if y is None else y + h
        o_ref[pl.ds(r0, CHUNK), :] = (y * _gelu_tanh(lg_ref[pl.ds(r0, CHUNK), :])).astype(o_ref.dtype)
        return carry

    lax.fori_loop(0, nch, emit, 0, unroll=2)


def _lru(p, conv_w, conv_b, w4, b4, lam, keep, h0):
    nt = N_TOK // LRU_TILE
    nch = LRU_TILE_CHUNKS
    lx_blk = COL_LX // LRU_BLK
    lg_blk = COL_LG // LRU_BLK
    return pl.pallas_call(
        _lru_kernel,
        grid=(LRU_BLOCKS, nt),
        in_specs=[
            pl.BlockSpec((LRU_TILE, LRU_BLK), lambda n, r: (r, lx_blk + n)),
            pl.BlockSpec((LRU_TILE, LRU_BLK), lambda n, r: (r, lg_blk + n)),
            pl.BlockSpec((CONV_W, LRU_BLK), lambda n, r: (0, n)),
            pl.BlockSpec((1, LRU_BLK), lambda n, r: (0, n)),
            pl.BlockSpec((None, LRU_BLK, 4 * LRU_BLK), lambda n, r: (n, 0, 0)),
            pl.BlockSpec((None, 1, 4 * LRU_BLK), lambda n, r: (n, 0, 0)),
            pl.BlockSpec((2, LRU_BLK), lambda n, r: (0, n)),
            pl.BlockSpec((2, nch, LRU_BLK), lambda n, r: (0, r, 0)),
            pl.BlockSpec((2, nch, LRU_BLK), lambda n, r: (0, r, n)),
        ],
        out_specs=[
            pl.BlockSpec((LRU_TILE, LRU_BLK), lambda n, r: (r, n)),
            pl.BlockSpec((2, nch, LRU_BLK), lambda n, r: (0, r, n)),
        ],
        out_shape=[
            jax.ShapeDtypeStruct((N_TOK, D_LRU), BF16),
            jax.ShapeDtypeStruct((2, N_CHUNKS, D_LRU), F32),
        ],
        scratch_shapes=[
            pltpu.VMEM((2, nch * CHUNK_PITCH, LANES), F32),
            pltpu.VMEM((2, nch * CHUNK_PITCH, LANES), F32),
            pltpu.VMEM((2, nch, LANES), F32),
            pltpu.VMEM((2, nch, LANES), F32),
            pltpu.VMEM((2, nch, LANES), F32),
        ],
        compiler_params=_params("arbitrary", "arbitrary"),
        name="rglru",
    )(p, p, conv_w, conv_b, w4, b4, lam, keep, h0)


def _lru_chain_inputs(state_l):
    c = jnp.arange(N_CHUNKS)
    ctx_per_seq = SEQ // CHUNK
    lat_per_seq = DEC_SEQ // CHUNK
    is_ctx = c < CTX_CHUNKS
    idx = jnp.where(is_ctx, c % ctx_per_seq, (c - CTX_CHUNKS) % lat_per_seq)
    per = jnp.where(is_ctx, ctx_per_seq, lat_per_seq)
    start = jnp.stack([idx == 0, idx == per - 1])
    keep = jnp.broadcast_to(jnp.where(start, 0.0, 1.0)[:, :, None], (2, N_CHUNKS, LANES)).astype(F32)
    lat_b = jnp.clip((c - CTX_CHUNKS) // lat_per_seq, 0, DEC_BATCH - 1)
    seed = jnp.transpose(state_l, (1, 0, 2))[:, lat_b, :]
    h0 = jnp.where((start & ~is_ctx[None, :])[:, :, None], seed, 0.0).astype(F32)
    return keep, h0


def _pool_kernel(x_ref, w_ref, sc_ref, o_ref):
    tile = pl.program_id(0)
    ext_rows = POOL_CHUNK + 2 * POOL_HALO
    is_ctx = tile * POOL_TILE < N_CTX
    erow = lax.broadcasted_iota(jnp.int32, (ext_rows, LANES), 0)
    row = lax.broadcasted_iota(jnp.int32, (POOL_CHUNK, LANES), 0)

    def shift_up(x, k):
        return x if k == 0 else pltpu.roll(x, ext_rows - k, 0)

    for g, win in enumerate(POOL_WINDOWS):
        half = win // 2
        cols = slice(g * POOL_GROUP, (g + 1) * POOL_GROUP)
        w = w_ref[g]
        sc = sc_ref[:, cols]

        def body(c, carry):
            r0 = pl.multiple_of(c * POOL_CHUNK, POOL_CHUNK)
            seq_len = jnp.where(is_ctx, SEQ, DEC_SEQ)
            t0 = jnp.where(is_ctx, 0, (tile * POOL_TILE + r0) & (DEC_SEQ - 1))
            first = t0 == 0
            last = t0 + POOL_CHUNK == seq_len
            lo = pl.multiple_of(jnp.maximum(r0 - POOL_HALO, 0), POOL_HALO)
            hi = pl.multiple_of(jnp.minimum(r0 + POOL_CHUNK, POOL_TILE - POOL_HALO), POOL_HALO)
            cur = x_ref[pl.ds(r0, POOL_CHUNK), cols]
            ext = jnp.concatenate(
                [x_ref[pl.ds(lo, POOL_HALO), cols], cur, x_ref[pl.ds(hi, POOL_HALO), cols]], axis=0)
            outside = jnp.logical_or(jnp.logical_and(first, erow < POOL_HALO),
                                     jnp.logical_and(last, erow >= POOL_HALO + POOL_CHUNK))
            acc = jnp.where(outside, 0.0, ext)
            span = 1
            while span < win:
                acc = acc + shift_up(acc, span)
                span *= 2
            total = shift_up(acc, POOL_HALO - half)[:POOL_CHUNK]
            t = t0 + row
            cnt = jnp.minimum(t + half, seq_len) - jnp.maximum(t - half, 0)
            pooled = total / cnt.astype(F32) - cur
            y = _dot(pooled.astype(BF16), w) * sc
            o_ref[pl.ds(r0, POOL_CHUNK), cols] = y.astype(o_ref.dtype)
            return carry

        lax.fori_loop(0, POOL_TILE // POOL_CHUNK, body, 0, unroll=2)


def _pool(p, pool_w_bf16, pool_scale):
    blk = COL_PU // D_POOL
    return pl.pallas_call(
        _pool_kernel,
        grid=(N_TOK // POOL_TILE,),
        in_specs=[
            pl.BlockSpec((POOL_TILE, D_POOL), lambda i: (i, blk)),
            pl.BlockSpec((len(POOL_WINDOWS), POOL_GROUP, POOL_GROUP), lambda i: (0, 0, 0)),
            pl.BlockSpec((1, D_POOL), lambda i: (0, 0)),
        ],
        out_specs=pl.BlockSpec((POOL_TILE, D_POOL), lambda i: (i, 0)),
        out_shape=jax.ShapeDtypeStruct((N_TOK, D_POOL), BF16),
        compiler_params=_params("arbitrary"),
        name="pool_mixer",
    )(p, pool_w_bf16, pool_scale)


OUTPROJ_TM = 512


def _outproj_kernel(attn_ref, lru_ref, pool_ref, *refs, two_x):
    if two_x:
        xa_ref, xb_ref, gate_ref, wa_ref, wl_ref, wp_ref, o_ref = refs
        x = jnp.where(pl.program_id(0) < N_CTX // OUTPROJ_TM, xa_ref[...], xb_ref[...])
    else:
        x_ref, gate_ref, wa_ref, wl_ref, wp_ref, o_ref = refs
        x = x_ref[...]
    mix = _dot(attn_ref[...], wa_ref[...]) + _dot(lru_ref[...], wl_ref[...]) + _dot(pool_ref[...], wp_ref[...])
    o_ref[...] = x + gate_ref[...] * mix


def _outproj(attn, lru, pool, xs, mod, w_out_bf16):
    tm = OUTPROJ_TM
    nt = N_TOK // tm
    n_ctx_tiles = N_CTX // tm
    two_x = len(xs) == 2
    if two_x:
        x_specs = [pl.BlockSpec((tm, D_MODEL), lambda i: (jnp.minimum(i, n_ctx_tiles - 1), 0)),
                   pl.BlockSpec((tm, D_MODEL), lambda i: (jnp.maximum(i - n_ctx_tiles, 0), 0))]
    else:
        x_specs = [pl.BlockSpec((tm, D_MODEL), lambda i: (i, 0))]
    return pl.pallas_call(
        functools.partial(_outproj_kernel, two_x=two_x),
        grid=(nt,),
        in_specs=[
            pl.BlockSpec((tm, D_ATTN), lambda i: (i, 0)),
            pl.BlockSpec((tm, D_LRU), lambda i: (i, 0)),
            pl.BlockSpec((tm, D_POOL), lambda i: (i, 0)),
        ] + x_specs + [
            pl.BlockSpec((None, 1, D_MODEL), lambda i: (_group_of_tile(i, tm) * N_MOD + 2, 0, 0)),
            pl.BlockSpec((D_ATTN, D_MODEL), lambda i: (0, 0)),
            pl.BlockSpec((D_LRU, D_MODEL), lambda i: (D_ATTN // D_LRU, 0)),
            pl.BlockSpec((D_POOL, D_MODEL), lambda i: ((D_ATTN + D_LRU) // D_POOL, 0)),
        ],
        out_specs=pl.BlockSpec((tm, D_MODEL), lambda i: (i, 0)),
        out_shape=jax.ShapeDtypeStruct((N_TOK, D_MODEL), F32),
        compiler_params=_params("arbitrary"),
        name="outproj",
    )(attn, lru, pool, *xs, mod, w_out_bf16, w_out_bf16, w_out_bf16)


FFN_TM = 1024
FFN_TF = 512
FFN_TN = 512
FFN_CAST_COLS = 2 * FFN_TF


def _ffn_kernel(x_ref, g_ref, shift_ref, scale_ref, gate_ref, wg_ref, wu_ref, w2_ref, fg_ref, *rest,
                final_norm, cast_next):
    if cast_next:
        w1n_f32_ref, w2n_f32_ref, o_ref, w1n_ref, w2n_ref, h_ref = rest
    else:
        o_ref, h_ref = rest
    i, j = pl.program_id(0), pl.program_id(1)

    @pl.when(j == 0)
    def _():
        _norm_mod_store(h_ref, x_ref, g_ref[...], scale_ref[...], shift_ref[...], FFN_TM, copy_ref=o_ref)

    h = h_ref[...]
    act = (_silu(_dot(h, wg_ref[...])) * _dot(h, wu_ref[...])).astype(BF16)
    for n in range(D_MODEL // FFN_TN):
        cols = slice(n * FFN_TN, (n + 1) * FFN_TN)
        o_ref[:, cols] += gate_ref[:, cols] * _dot(act, w2_ref[:, cols])

    if cast_next:
        @pl.when(i < CAST_TILES)
        def _():
            w1n_ref[...] = w1n_f32_ref[...].astype(BF16)
            w2n_ref[...] = w2n_f32_ref[...].astype(BF16)

    if final_norm:
        @pl.when(j == pl.num_programs(1) - 1)
        def _():
            gain = fg_ref[...]

            def body(r, carry):
                r0 = pl.multiple_of(r * NORM_ROWS, NORM_ROWS)
                x = o_ref[pl.ds(r0, NORM_ROWS), :]
                ms = jnp.mean(x * x, axis=-1, keepdims=True)
                o_ref[pl.ds(r0, NORM_ROWS), :] = x * lax.rsqrt(ms + RMS_EPS) * gain
                return carry

            lax.fori_loop(0, FFN_TM // NORM_ROWS, body, 0, unroll=NORM_UNROLL)


def _ffn(x, norm_g, mod, w1_l, w2_l, final_g, layer, tile0, ntiles, final_norm, w1_f32=None, w2_f32=None):
    tm, tf = FFN_TM, FFN_TF
    nf = D_FF // tf
    cast_next = w1_f32 is not None

    def mod_spec(which):
        return pl.BlockSpec((None, 1, D_MODEL),
                            lambda i, j: (_group_of_tile(tile0 + i, tm) * N_MOD + which, 0, 0))

    in_specs = [
        pl.BlockSpec((tm, D_MODEL), lambda i, j: (tile0 + i, 0), pipeline_mode=pl.Buffered(1)),
        pl.BlockSpec((1, D_MODEL), lambda i, j: (0, 0)),
        mod_spec(3),
        mod_spec(4),
        mod_spec(5),
        pl.BlockSpec((D_MODEL, tf), lambda i, j: (0, j)),
        pl.BlockSpec((D_MODEL, tf), lambda i, j: (0, nf + j)),
        pl.BlockSpec((tf, D_MODEL), lambda i, j: (j, 0)),
        pl.BlockSpec((1, D_MODEL), lambda i, j: (0, 0)),
    ]
    out_specs = [pl.BlockSpec((tm, D_MODEL), lambda i, j: (i, 0))]
    out_shape = [jax.ShapeDtypeStruct((ntiles * tm, D_MODEL), F32)]
    args = [x, norm_g, mod, mod, mod, w1_l, w1_l, w2_l, final_g]
    if cast_next:
        assert ntiles >= CAST_TILES and 2 * D_FF == nf * FFN_CAST_COLS
        nxt = layer + 1
        w1_idx = lambda i, j: _cast_idx(i, j, nf - 1)
        w2_idx = lambda i, j: _cast_idx(i, j, nf - 1)[::-1]
        in_specs += [
            pl.BlockSpec((None, CAST_ROWS, FFN_CAST_COLS), lambda i, j: (nxt,) + w1_idx(i, j)),
            pl.BlockSpec((None, tf, CAST_ROWS), lambda i, j: (nxt,) + w2_idx(i, j)),
        ]
        out_specs += [pl.BlockSpec((CAST_ROWS, FFN_CAST_COLS), w1_idx), pl.BlockSpec((tf, CAST_ROWS), w2_idx)]
        out_shape += [jax.ShapeDtypeStruct((D_MODEL, 2 * D_FF), BF16), jax.ShapeDtypeStruct((D_FF, D_MODEL), BF16)]
        args += [w1_f32, w2_f32]
    out = pl.pallas_call(
        functools.partial(_ffn_kernel, final_norm=final_norm, cast_next=cast_next),
        grid=(ntiles, nf),
        in_specs=in_specs,
        out_specs=out_specs,
        out_shape=out_shape,
        scratch_shapes=[pltpu.VMEM((tm, D_MODEL), BF16)],
        compiler_params=_params("arbitrary", "arbitrary"),
        name="ffn_final" if final_norm else ("ffn_cast" if cast_next else "ffn"),
    )(*args)
    return out if cast_next else out[0]


def kernel(x_prompt, x_sample, cache_k, cache_v, state_lru, c, c_ctx, mod_w, mod_b, norm_mix, norm_ffn,
           w_in, attn_sink, conv_w, conv_b, lru_wa, lru_ba, lru_wx, lru_bx, lru_lambda, pool_w,
           pool_scale, w_out, ffn_w1, ffn_w2, norm_final):
    xs = (x_prompt.reshape(N_CTX, D_MODEL), x_sample.reshape(N_LAT, D_MODEL))

    cond = jnp.zeros((COND_ROWS, D_MODEL), F32).at[0].set(c_ctx).at[1:1 + DEC_BATCH].set(c)
    mod_all = _mod_vectors(cond, mod_w, mod_b)
    mod_all = mod_all[:, :N_GROUPS].reshape(DEPTH, N_GROUPS * N_MOD, 1, D_MODEL)

    w_in_l = w_in[0].astype(BF16)
    w_out_l = w_out[0].astype(BF16)
    w1_l = ffn_w1[0].astype(BF16)
    w2_l = ffn_w2[0].astype(BF16)
    pool_w_b = pool_w.astype(BF16)
    w4 = (0.5 * jnp.concatenate([lru_wa[:, 0], lru_wx[:, 0], lru_wa[:, 1], lru_wx[:, 1]], axis=-1)).astype(BF16)
    ba = lru_ba.reshape(DEPTH, 2, LRU_BLOCKS, LRU_BLK)
    bx = lru_bx.reshape(DEPTH, 2, LRU_BLOCKS, LRU_BLK)
    b4 = 0.5 * jnp.stack([ba[:, 0], bx[:, 0], ba[:, 1], bx[:, 1]], axis=2).reshape(DEPTH, LRU_BLOCKS, 1, 4 * LRU_BLK)
    cache_k2 = cache_k.reshape(DEC_BATCH, DEPTH, PAST_LEN, D_KV)
    cache_v2 = cache_v.reshape(DEC_BATCH, DEPTH, PAST_LEN, D_KV)
    rope_tabs = _rope_tables()

    kc = jnp.zeros((BATCH, DEPTH, SEQ, D_KV), F32)
    vc = jnp.zeros((BATCH, DEPTH, SEQ, D_KV), F32)
    attn = jnp.zeros((N_TOK, D_ATTN), BF16)
    hs_new = []
    for l in range(DEPTH):
        mod = mod_all[l]
        last = l == DEPTH - 1
        w_out_cur = w_out_l
        if last:
            p, kc, vc = _inproj(xs, norm_mix[l][None, :], mod, w_in_l, kc, vc, l)
        else:
            p, kc, vc, w_in_l, w_out_l = _inproj(xs, norm_mix[l][None, :], mod, w_in_l, kc, vc, l, w_in, w_out)

        attn = _ctx_attention(p, attn_sink[l], attn)
        attn = _lat_attention(p, attn_sink[l], cache_k2, cache_v2, rope_tabs, attn, l)
        keep, h0 = _lru_chain_inputs(state_lru[:, l])
        lru, hend = _lru(p, conv_w[l], conv_b[l][None, :], w4[l], b4[l], lru_lambda[l], keep, h0)
        pool = _pool(p, pool_w_b[l], pool_scale[l][None, :])

        ctx_per_seq = SEQ // CHUNK
        fwd_fin = hend[0, ctx_per_seq - 1:CTX_CHUNKS:ctx_per_seq]
        bwd_fin = hend[1, 0:CTX_CHUNKS:ctx_per_seq]
        hs_new.append(jnp.stack([fwd_fin, bwd_fin], axis=1))

        x = _outproj(attn, lru, pool, xs, mod, w_out_cur)
        ffn_args = (x, norm_ffn[l][None, :], mod, w1_l, w2_l, norm_final[None, :], l)
        if not last:
            x, w1_l, w2_l = _ffn(*ffn_args, 0, N_TOK // FFN_TM, False, ffn_w1, ffn_w2)
            xs = (x,)

    y_prompt = _ffn(*ffn_args, 0, N_CTX // FFN_TM, True).reshape(BATCH, SEQ, D_MODEL)
    y_sample = _ffn(*ffn_args, N_CTX // FFN_TM, N_LAT // FFN_TM, True).reshape(DEC_BATCH, DEC_SEQ, D_MODEL)
    cache_shape = (BATCH, DEPTH, SEQ, N_KV_HEADS, HEAD_DIM)
    return (y_prompt, y_sample, kc.reshape(cache_shape), vc.reshape(cache_shape), jnp.stack(hs_new, axis=1))
```

```python
import functools
import math

import jax
import jax.numpy as jnp
from jax import lax
from jax.experimental import pallas as pl
from jax.experimental.pallas import tpu as pltpu

F32 = jnp.float32
BF16 = jnp.bfloat16

D_MODEL = 2048
BATCH = 32
SEQ = 256
DEPTH = 4
DEC_BATCH = 2
DEC_SEQ = 2048
PAST_LEN = 512
GRID_W = 64
HEAD_DIM = 128
N_HEADS = 8
N_KV_HEADS = 2
KV_GROUP = N_HEADS // N_KV_HEADS
D_ATTN = N_HEADS * HEAD_DIM
D_KV = N_KV_HEADS * HEAD_DIM
WINDOW = 128
Q_BLOCK = 128
ROPE_THETA = 10000.0
D_LRU = D_MODEL // 4
LRU_BLOCKS = 4
LRU_BLK = D_LRU // LRU_BLOCKS
CONV_W = 4
LRU_C = 8.0
D_POOL = D_MODEL // 4
POOL_WINDOWS = (2, 4, 8, 16)
POOL_GROUP = D_POOL // len(POOL_WINDOWS)
D_IN = D_ATTN + 2 * D_KV + 2 * D_LRU + D_POOL
D_FF = 5632
N_MOD = 6
RMS_EPS = 1e-6
NEG_INF = -1e30

N_CTX = BATCH * SEQ
N_LAT = DEC_BATCH * DEC_SEQ
N_TOK = N_CTX + N_LAT
N_GROUPS = 1 + DEC_BATCH
COND_ROWS = 16

COL_Q = 0
COL_K = D_ATTN
COL_V = D_ATTN + D_KV
COL_LX = D_ATTN + 2 * D_KV
COL_LG = COL_LX + D_LRU
COL_PU = COL_LG + D_LRU

VMEM_LIMIT = 56 * 1024 * 1024
MATMUL_STAGING_BYTES = 6 * 1024 * 1024

NORM_ROWS = 32
NORM_UNROLL = 4
LANES = 128
SUBLANES = 8

CHUNK = 128
CHUNK_PITCH = 136
SQRT_TINY = 1e-30
SCAN_UNROLL = 4
LRU_TILE_CHUNKS = 48
LRU_TILE = LRU_TILE_CHUNKS * CHUNK
N_CHUNKS = N_TOK // CHUNK
CTX_CHUNKS = N_CTX // CHUNK

POOL_TILE = DEC_SEQ
POOL_CHUNK = 256
POOL_HALO = 8


def _dot(a, b):
    return jnp.dot(a, b, preferred_element_type=F32)


def _dot_nt(a, b):
    return lax.dot_general(a, b, (((1,), (1,)), ((), ())), preferred_element_type=F32)


def _sigmoid(x):
    return 1.0 / (1.0 + jnp.exp(-x))


def _silu(x):
    return x * _sigmoid(x)


def _gelu_tanh(x):
    c = math.sqrt(2.0 / math.pi)
    return 0.5 * x * (1.0 + jnp.tanh(c * (x + 0.044715 * (x * x * x))))


def _params(*sem, vmem_limit=VMEM_LIMIT):
    return pltpu.CompilerParams(dimension_semantics=sem, vmem_limit_bytes=vmem_limit)


CAST_TILES = 8
CAST_ROWS = D_MODEL // CAST_TILES


def _group_of_tile(i, tm):
    n_ctx_tiles = N_CTX // tm
    tiles_per_lat = DEC_SEQ // tm
    return jnp.where(i < n_ctx_tiles, 0, 1 + (i - n_ctx_tiles) // tiles_per_lat)


MOD_TN = 1024


def _mod_kernel(cond_ref, w_ref, b_ref, o_ref):
    s = _silu(cond_ref[...]).astype(BF16)
    o_ref[...] = _dot(s, w_ref[...].astype(BF16)) + b_ref[...]


def _mod_vectors(cond, mod_w, mod_b):
    n = N_MOD * D_MODEL
    return pl.pallas_call(
        _mod_kernel,
        grid=(DEPTH, n // MOD_TN),
        in_specs=[
            pl.BlockSpec((COND_ROWS, D_MODEL), lambda l, j: (0, 0)),
            pl.BlockSpec((None, D_MODEL, MOD_TN), lambda l, j: (l, 0, j)),
            pl.BlockSpec((None, 1, MOD_TN), lambda l, j: (l, 0, j)),
        ],
        out_specs=pl.BlockSpec((None, COND_ROWS, MOD_TN), lambda l, j: (l, 0, j)),
        out_shape=jax.ShapeDtypeStruct((DEPTH, COND_ROWS, n), F32),
        compiler_params=_params("arbitrary", "arbitrary"),
        name="mod_vectors",
    )(cond, mod_w, mod_b.reshape(DEPTH, 1, n))


INPROJ_TM = 512
INPROJ_TILES = N_TOK // INPROJ_TM
INPROJ_CAST_COLS = 1024


def _inproj_kernel(*refs, cast_next, two_x):
    refs = list(refs)
    xa_ref = refs.pop(0)
    xb_ref = refs.pop(0) if two_x else None
    g_ref, shift_ref, scale_ref, w_ref, kc_in_ref, vc_in_ref = refs[:6]
    del kc_in_ref, vc_in_ref
    refs = refs[6:]
    if cast_next:
        win_f32_ref, wout_f32_ref, o_ref, kc_ref, vc_ref, win_ref, wout_ref, h0_ref, h1_ref = refs
    else:
        o_ref, kc_ref, vc_ref, h0_ref, h1_ref = refs
    s = pl.program_id(0)
    odd = s % 2 == 1

    gain = g_ref[...] * (1.0 + scale_ref[...])
    shift = shift_ref[...]
    from_a = s < N_CTX // INPROJ_TM

    def norm_into(h_write):
        for c in range(INPROJ_TM // NORM_ROWS):
            rows = slice(c * NORM_ROWS, (c + 1) * NORM_ROWS)
            x = xa_ref[rows, :]
            if two_x:
                x = jnp.where(from_a, x, xb_ref[rows, :])
            ms = jnp.mean(x * x, axis=-1, keepdims=True)
            h_write[rows, :] = (x * lax.rsqrt(ms + RMS_EPS) * gain + shift).astype(BF16)

    def step(h_write, h_read):
        norm_into(h_write)
        o_ref[...] = _dot(h_read[...], w_ref[...])

    @pl.when(s == 0)
    def _():
        norm_into(h0_ref)

    @pl.when(jnp.logical_and(s > 0, jnp.logical_not(odd)))
    def _():
        step(h0_ref, h1_ref)

    @pl.when(odd)
    def _():
        step(h1_ref, h0_ref)

    @pl.when(jnp.logical_and(s >= 1, s <= N_CTX // INPROJ_TM))
    def _():
        kc_ref[...] = o_ref[:, COL_K:COL_K + D_KV].reshape(kc_ref.shape)
        vc_ref[...] = o_ref[:, COL_V:COL_V + D_KV].reshape(vc_ref.shape)

    if cast_next:
        @pl.when(s < CAST_TILES * (D_IN // INPROJ_CAST_COLS))
        def _():
            win_ref[...] = win_f32_ref[...].astype(BF16)

        @pl.when(s < CAST_TILES * (D_MODEL // INPROJ_CAST_COLS))
        def _():
            wout_ref[...] = wout_f32_ref[...].astype(BF16)


def _inproj(xs, norm_g, mod, w_in_l, kc, vc, layer, w_in_f32=None, w_out_f32=None):
    tm = INPROJ_TM
    nt = INPROJ_TILES
    cast_next = w_in_f32 is not None
    two_x = len(xs) == 2
    seqs = tm // SEQ
    n_ctx_tiles = N_CTX // tm

    def norm_tile(s):
        return jnp.minimum(s, nt - 1)

    def mm_tile(s):
        return jnp.maximum(s - 1, 0)

    def mod_spec(which):
        return pl.BlockSpec((None, 1, D_MODEL), lambda s: (_group_of_tile(norm_tile(s), tm) * N_MOD + which, 0, 0))

    def cast_idx(ncols):
        def idx(s):
            t = jnp.minimum(s, CAST_TILES * ncols - 1)
            return t // ncols, t % ncols
        return idx

    cache_spec = pl.BlockSpec((seqs, None, SEQ, D_KV),
                              lambda s: (jnp.minimum(mm_tile(s), n_ctx_tiles - 1), layer, 0, 0))
    cache_shape = jax.ShapeDtypeStruct((BATCH, DEPTH, SEQ, D_KV), F32)
    if two_x:
        x_specs = [pl.BlockSpec((tm, D_MODEL), lambda s: (jnp.minimum(s, n_ctx_tiles - 1), 0)),
                   pl.BlockSpec((tm, D_MODEL), lambda s: (jnp.clip(s - n_ctx_tiles, 0, nt - n_ctx_tiles - 1), 0))]
    else:
        x_specs = [pl.BlockSpec((tm, D_MODEL), lambda s: (norm_tile(s), 0))]
    in_specs = x_specs + [
        pl.BlockSpec((1, D_MODEL), lambda s: (0, 0)),
        mod_spec(0),
        mod_spec(1),
        pl.BlockSpec((D_MODEL, D_IN), lambda s: (0, 0), pipeline_mode=pl.Buffered(1)),
        pl.BlockSpec(memory_space=pl.ANY),
        pl.BlockSpec(memory_space=pl.ANY),
    ]
    out_specs = [pl.BlockSpec((tm, D_IN), lambda s: (mm_tile(s), 0)), cache_spec, cache_spec]
    out_shape = [jax.ShapeDtypeStruct((N_TOK, D_IN), F32), cache_shape, cache_shape]
    args = list(xs) + [norm_g, mod, mod, w_in_l, kc, vc]
    n_in = len(args)
    if cast_next:
        nxt = layer + 1
        cc = INPROJ_CAST_COLS
        win_idx = cast_idx(D_IN // cc)
        wout_idx = cast_idx(D_MODEL // cc)
        in_specs += [
            pl.BlockSpec((None, CAST_ROWS, cc), lambda s: (nxt,) + win_idx(s)),
            pl.BlockSpec((None, CAST_ROWS, cc), lambda s: (nxt,) + wout_idx(s)),
        ]
        out_specs += [pl.BlockSpec((CAST_ROWS, cc), win_idx), pl.BlockSpec((CAST_ROWS, cc), wout_idx)]
        out_shape += [jax.ShapeDtypeStruct((D_MODEL, D_IN), BF16), jax.ShapeDtypeStruct((D_MODEL, D_MODEL), BF16)]
        args += [w_in_f32, w_out_f32]
    cast_bytes = 2 * 2 * CAST_ROWS * INPROJ_CAST_COLS * (4 + 2) if cast_next else 0
    vmem_bytes = (D_MODEL * D_IN * 2 + 2 * len(xs) * tm * D_MODEL * 4 + 2 * tm * D_IN * 4 + 2 * tm * D_MODEL * 2
                  + 2 * 2 * tm * D_KV * 4 + cast_bytes + MATMUL_STAGING_BYTES)
    return pl.pallas_call(
        functools.partial(_inproj_kernel, cast_next=cast_next, two_x=two_x),
        grid=(nt + 1,),
        in_specs=in_specs,
        out_specs=out_specs,
        out_shape=out_shape,
        input_output_aliases={n_in - 2: 1, n_in - 1: 2},
        scratch_shapes=[pltpu.VMEM((tm, D_MODEL), BF16), pltpu.VMEM((tm, D_MODEL), BF16)],
        compiler_params=_params("arbitrary", vmem_limit=vmem_bytes),
        name="inproj_cast" if cast_next else "inproj",
    )(*args)


def _sink_column(sink_ref, kv_head, rows_per_head):
    n = KV_GROUP * rows_per_head
    row = lax.broadcasted_iota(jnp.int32, (n, 1), 0)
    col = jnp.full((n, 1), sink_ref[kv_head * KV_GROUP + KV_GROUP - 1], F32)
    for hh in range(KV_GROUP - 2, -1, -1):
        col = jnp.where(row < (hh + 1) * rows_per_head, sink_ref[kv_head * KV_GROUP + hh], col)
    return col


def _stack_heads(q, kv_head):
    return jnp.concatenate(
        [q[:, (kv_head * KV_GROUP + hh) * HEAD_DIM:(kv_head * KV_GROUP + hh + 1) * HEAD_DIM]
         for hh in range(KV_GROUP)], axis=0)


def _ctx_attn_kernel(sink_ref, q_ref, kv_ref, buf_hbm_ref, o_ref):
    del buf_hbm_ref
    scale = HEAD_DIM ** -0.5
    q = q_ref[...]
    kv = kv_ref[...]
    ones = jnp.ones((SEQ, HEAD_DIM), BF16)
    for g in range(N_KV_HEADS):
        k = kv[:, g * HEAD_DIM:(g + 1) * HEAD_DIM].astype(BF16)
        v = kv[:, D_KV + g * HEAD_DIM:D_KV + (g + 1) * HEAD_DIM].astype(BF16)
        qs = (_stack_heads(q, g) * scale).astype(BF16)
        s = _dot_nt(qs, k)
        sink_col = _sink_column(sink_ref, g, SEQ)
        m = jnp.maximum(jnp.max(s, axis=-1, keepdims=True), sink_col)
        p = jnp.exp(s - m).astype(BF16)
        pv = _dot(p, jnp.concatenate([v, ones], axis=1))
        o = pv[:, :HEAD_DIM] / (pv[:, HEAD_DIM:] + jnp.exp(sink_col - m))
        for hh in range(KV_GROUP):
            h = g * KV_GROUP + hh
            o_ref[:, h * HEAD_DIM:(h + 1) * HEAD_DIM] = o[hh * SEQ:(hh + 1) * SEQ].astype(o_ref.dtype)


def _ctx_attention(p, sink, buf):
    kv_blk = 2 * D_KV
    return pl.pallas_call(
        _ctx_attn_kernel,
        grid_spec=pltpu.PrefetchScalarGridSpec(
            num_scalar_prefetch=1,
            grid=(BATCH,),
            in_specs=[
                pl.BlockSpec((SEQ, D_ATTN), lambda b, s: (b, 0)),
                pl.BlockSpec((SEQ, kv_blk), lambda b, s: (b, COL_K // kv_blk)),
                pl.BlockSpec(memory_space=pl.ANY),
            ],
            out_specs=pl.BlockSpec((SEQ, D_ATTN), lambda b, s: (b, 0)),
        ),
        out_shape=jax.ShapeDtypeStruct((N_TOK, D_ATTN), BF16),
        input_output_aliases={3: 0},
        compiler_params=_params("arbitrary"),
        name="ctx_attention",
    )(sink, p, p, buf)


def _rope(x, cos, sin_lo, sin_hi):
    return x * cos + pltpu.roll(x, 96, 1) * sin_lo + pltpu.roll(x, 32, 1) * sin_hi


def _lat_attn_kernel(sink_ref, q_ref, kv_ref, ck_ref, cv_ref, cos_ref, slo_ref, shi_ref, attn_hbm_ref,
                     o_ref, kt_s, v_s, ckt_s, cv_s):
    del attn_hbm_ref
    i = pl.program_id(1)
    scale = HEAD_DIM ** -0.5
    band = Q_BLOCK + 2 * WINDOW
    band_blocks = band // Q_BLOCK

    @pl.when(i == 0)
    def _():
        for g in range(N_KV_HEADS):
            sl = slice(g * HEAD_DIM, (g + 1) * HEAD_DIM)
            k = _rope(kv_ref[:, sl], cos_ref[...], slo_ref[...], shi_ref[...])
            for t in range(DEC_SEQ // Q_BLOCK):
                kt_s[g, t] = k[t * Q_BLOCK:(t + 1) * Q_BLOCK].T.astype(BF16)
            ckt_s[g] = ck_ref[:, sl].T.astype(BF16)
        v_s[...] = kv_ref[:, D_KV:2 * D_KV].astype(BF16)
        cv_s[...] = cv_ref[...].astype(BF16)

    q0 = pl.multiple_of(i * Q_BLOCK, Q_BLOCK)
    kb0 = jnp.clip(i - WINDOW // Q_BLOCK, 0, DEC_SEQ // Q_BLOCK - band_blocks)
    k0 = pl.multiple_of(kb0 * Q_BLOCK, Q_BLOCK)
    cos = cos_ref[pl.ds(q0, Q_BLOCK), :] * scale
    slo = slo_ref[pl.ds(q0, Q_BLOCK), :] * scale
    shi = shi_ref[pl.ds(q0, Q_BLOCK), :] * scale
    cos4 = jnp.concatenate([cos] * KV_GROUP, axis=0)
    slo4 = jnp.concatenate([slo] * KV_GROUP, axis=0)
    shi4 = jnp.concatenate([shi] * KV_GROUP, axis=0)

    qpos = q0 + lax.broadcasted_iota(jnp.int32, (Q_BLOCK, band), 0)
    kpos = k0 + lax.broadcasted_iota(jnp.int32, (Q_BLOCK, band), 1)
    bias1 = jnp.where(jnp.abs(qpos - kpos) <= WINDOW, 0.0, NEG_INF).astype(F32)
    bias = jnp.concatenate([bias1] * KV_GROUP, axis=0)
    ones = jnp.ones((band + PAST_LEN, HEAD_DIM), BF16)

    q = q_ref[...]
    for g in range(N_KV_HEADS):
        sl = slice(g * HEAD_DIM, (g + 1) * HEAD_DIM)
        qs = _rope(_stack_heads(q, g), cos4, slo4, shi4).astype(BF16)
        kt = jnp.concatenate([kt_s[g, kb0 + t] for t in range(band_blocks)] + [ckt_s[g]], axis=1)
        s = _dot(qs, kt)
        s_band = s[:, :band] + bias
        s_ctx = s[:, band:]
        sink_col = _sink_column(sink_ref, g, Q_BLOCK)
        m = jnp.maximum(jnp.maximum(jnp.max(s_band, axis=-1, keepdims=True),
                                    jnp.max(s_ctx, axis=-1, keepdims=True)), sink_col)
        p = jnp.concatenate([jnp.exp(s_band - m), jnp.exp(s_ctx - m)], axis=1).astype(BF16)
        v = jnp.concatenate([v_s[pl.ds(k0, band), sl], cv_s[:, sl]], axis=0)
        pv = _dot(p, jnp.concatenate([v, ones], axis=1))
        o = pv[:, :HEAD_DIM] / (pv[:, HEAD_DIM:] + jnp.exp(sink_col - m))
        for hh in range(KV_GROUP):
            h = g * KV_GROUP + hh
            o_ref[:, h * HEAD_DIM:(h + 1) * HEAD_DIM] = (
                o[hh * Q_BLOCK:(hh + 1) * Q_BLOCK].astype(o_ref.dtype))


def _lat_attention(p, sink, cache_k, cache_v, rope_tabs, attn, layer):
    kv_blk = 2 * D_KV
    nq = DEC_SEQ // Q_BLOCK
    ctx_rows = N_CTX // Q_BLOCK
    seq_blk = N_CTX // DEC_SEQ
    tab_spec = pl.BlockSpec((DEC_SEQ, HEAD_DIM), lambda b, i, s: (0, 0))
    cache_spec = pl.BlockSpec((None, None, PAST_LEN, D_KV), lambda b, i, s: (b, layer, 0, 0))
    return pl.pallas_call(
        _lat_attn_kernel,
        grid_spec=pltpu.PrefetchScalarGridSpec(
            num_scalar_prefetch=1,
            grid=(DEC_BATCH, nq),
            in_specs=[
                pl.BlockSpec((Q_BLOCK, D_ATTN), lambda b, i, s: (ctx_rows + b * nq + i, 0)),
                pl.BlockSpec((DEC_SEQ, kv_blk), lambda b, i, s: (seq_blk + b, COL_K // kv_blk)),
                cache_spec, cache_spec, tab_spec, tab_spec, tab_spec,
                pl.BlockSpec(memory_space=pl.ANY),
            ],
            out_specs=pl.BlockSpec((Q_BLOCK, D_ATTN), lambda b, i, s: (ctx_rows + b * nq + i, 0)),
            scratch_shapes=[
                pltpu.VMEM((N_KV_HEADS, DEC_SEQ // Q_BLOCK, HEAD_DIM, Q_BLOCK), BF16),
                pltpu.VMEM((DEC_SEQ, D_KV), BF16),
                pltpu.VMEM((N_KV_HEADS, HEAD_DIM, PAST_LEN), BF16),
                pltpu.VMEM((PAST_LEN, D_KV), BF16),
            ],
        ),
        out_shape=jax.ShapeDtypeStruct((N_TOK, D_ATTN), BF16),
        input_output_aliases={8: 0},
        compiler_params=_params("arbitrary", "arbitrary"),
        name="lat_attention",
    )(sink, p, p, cache_k, cache_v, *rope_tabs, attn)


def _rope_tables():
    t = jnp.arange(DEC_SEQ)
    pos = jnp.stack([t // GRID_W, t % GRID_W], axis=-1).astype(F32)
    rd = HEAD_DIM // 4
    inv = ROPE_THETA ** (-jnp.arange(rd, dtype=F32) / rd)
    ang = jnp.broadcast_to(pos[:, :, None, None] * inv, (DEC_SEQ, 2, 2, rd)).reshape(DEC_SEQ, HEAD_DIM)
    cos, sin = jnp.cos(ang), jnp.sin(ang)
    first = (jnp.arange(HEAD_DIM) % (2 * rd)) < rd
    return cos, jnp.where(first, -sin, 0.0), jnp.where(first, 0.0, sin)


def _seq_position(chunk):
    ctx_per_seq = SEQ // CHUNK
    lat_per_seq = DEC_SEQ // CHUNK
    is_ctx = chunk < CTX_CHUNKS
    idx = jnp.where(is_ctx, chunk % ctx_per_seq, (chunk - CTX_CHUNKS) % lat_per_seq)
    per = jnp.where(is_ctx, ctx_per_seq, lat_per_seq)
    return idx == 0, idx == per - 1


def _lru_kernel(lx_ref, lg_ref, cw_ref, cb_ref, w4_ref, b4_ref, lam_ref, keep_ref, h0_ref,
                o_ref, hend_ref, a_s, b_s, tot_s, loc_s, hin_s):
    tile = pl.program_id(1)
    nch = LRU_TILE_CHUNKS
    row = lax.broadcasted_iota(jnp.int32, (CHUNK, LANES), 0)
    lam = lam_ref[...]
    half_c = (0.5 * LRU_C) * (jnp.minimum(lam, 0.0) - jnp.log(1.0 + jnp.exp(-jnp.abs(lam))))
    cw = cw_ref[...]
    cb = cb_ref[...]
    w4 = w4_ref[...]
    b4 = b4_ref[...]

    def gates(c, carry):
        r0 = pl.multiple_of(c * CHUNK, CHUNK)
        first, last = _seq_position(tile * nch + c)
        lo = pl.multiple_of(jnp.maximum(r0 - SUBLANES, 0), SUBLANES)
        hi = pl.multiple_of(jnp.minimum(r0 + CHUNK, LRU_TILE - SUBLANES), SUBLANES)
        ext = jnp.concatenate(
            [lx_ref[pl.ds(lo, SUBLANES), :], lx_ref[pl.ds(r0, CHUNK), :], lx_ref[pl.ds(hi, SUBLANES), :]],
            axis=0)
        xm2 = jnp.where(jnp.logical_and(first, row < 2), 0.0, ext[6:6 + CHUNK])
        xm1 = jnp.where(jnp.logical_and(first, row < 1), 0.0, ext[7:7 + CHUNK])
        x0 = ext[8:8 + CHUNK]
        xp1 = jnp.where(jnp.logical_and(last, row >= CHUNK - 1), 0.0, ext[9:9 + CHUNK])
        u = xm2 * cw[0:1] + xm1 * cw[1:2] + x0 * cw[2:3] + xp1 * cw[3:4] + cb
        t = jnp.tanh(_dot(u.astype(BF16), w4) + b4)
        u_half = 0.5 * u
        s0 = pl.multiple_of(c * CHUNK_PITCH, SUBLANES)
        for d in range(2):
            t_r = t[:, (2 * d) * LANES:(2 * d + 1) * LANES]
            t_i = t[:, (2 * d + 1) * LANES:(2 * d + 2) * LANES]
            a = jnp.exp(t_r * half_c[d:d + 1] + half_c[d:d + 1])
            a_s[d, pl.ds(s0, CHUNK), :] = a
            x = 1.0 - a * a
            root = x * lax.rsqrt(jnp.maximum(x, SQRT_TINY))
            b_s[d, pl.ds(s0, CHUNK), :] = root * (t_i + 1.0) * u_half
        return carry

    lax.fori_loop(0, nch, gates, 0, unroll=2)

    def time_index(k, d):
        return pl.ds(k if d == 0 else CHUNK - 1 - k, nch, stride=CHUNK_PITCH)

    def local_scan(k, carry):
        out = []
        for d in range(2):
            a_t = a_s[d, time_index(k, d), :]
            acc_a = carry[2 * d] * a_t
            h = a_t * carry[2 * d + 1] + b_s[d, time_index(k, d), :]
            a_s[d, time_index(k, d), :] = acc_a
            b_s[d, time_index(k, d), :] = h
            out += [acc_a, h]
        return tuple(out)

    ones = jnp.ones((nch, LANES), F32)
    zeros = jnp.zeros((nch, LANES), F32)
    fin = lax.fori_loop(0, CHUNK, local_scan, (ones, zeros, ones, zeros), unroll=SCAN_UNROLL)
    for d in range(2):
        tot_s[d] = fin[2 * d]
        loc_s[d] = fin[2 * d + 1]

    for d in range(2):
        prev = jnp.zeros((1, LANES), F32)
        order = range(nch) if d == 0 else range(nch - 1, -1, -1)
        for c in order:
            hin = keep_ref[d, c:c + 1, :] * prev + h0_ref[d, c:c + 1, :]
            hin_s[d, c:c + 1, :] = hin
            prev = tot_s[d, c:c + 1, :] * hin + loc_s[d, c:c + 1, :]
            hend_ref[d, c:c + 1, :] = prev

    def emit(c, carry):
        r0 = pl.multiple_of(c * CHUNK, CHUNK)
        s0 = pl.multiple_of(c * CHUNK_PITCH, SUBLANES)
        y = None
        for d in range(2):
            h = b_s[d, pl.ds(s0, CHUNK), :] + a_s[d, pl.ds(s0, CHUNK), :] * hin_s[d, pl.ds(c, 1), :]
            y = h if y is None else y + h
        o_ref[pl.ds(r0, CHUNK), :] = (y * _gelu_tanh(lg_ref[pl.ds(r0, CHUNK), :])).astype(o_ref.dtype)
        return carry

    lax.fori_loop(0, nch, emit, 0, unroll=2)


def _lru(p, conv_w, conv_b, w4, b4, lam, keep, h0):
    nt = N_TOK // LRU_TILE
    nch = LRU_TILE_CHUNKS
    lx_blk = COL_LX // LRU_BLK
    lg_blk = COL_LG // LRU_BLK
    return pl.pallas_call(
        _lru_kernel,
        grid=(LRU_BLOCKS, nt),
        in_specs=[
            pl.BlockSpec((LRU_TILE, LRU_BLK), lambda n, r: (r, lx_blk + n)),
            pl.BlockSpec((LRU_TILE, LRU_BLK), lambda n, r: (r, lg_blk + n)),
            pl.BlockSpec((CONV_W, LRU_BLK), lambda n, r: (0, n)),
            pl.BlockSpec((1, LRU_BLK), lambda n, r: (0, n)),
            pl.BlockSpec((None, LRU_BLK, 4 * LRU_BLK), lambda n, r: (n, 0, 0)),
            pl.BlockSpec((None, 1, 4 * LRU_BLK), lambda n, r: (n, 0, 0)),
            pl.BlockSpec((2, LRU_BLK), lambda n, r: (0, n)),
            pl.BlockSpec((2, nch, LRU_BLK), lambda n, r: (0, r, 0)),
            pl.BlockSpec((2, nch, LRU_BLK), lambda n, r: (0, r, n)),
        ],
        out_specs=[
            pl.BlockSpec((LRU_TILE, LRU_BLK), lambda n, r: (r, n)),
            pl.BlockSpec((2, nch, LRU_BLK), lambda n, r: (0, r, n)),
        ],
        out_shape=[
            jax.ShapeDtypeStruct((N_TOK, D_LRU), BF16),
            jax.ShapeDtypeStruct((2, N_CHUNKS, D_LRU), F32),
        ],
        scratch_shapes=[
            pltpu.VMEM((2, nch * CHUNK_PITCH, LANES), F32),
            pltpu.VMEM((2, nch * CHUNK_PITCH, LANES), F32),
            pltpu.VMEM((2, nch, LANES), F32),
            pltpu.VMEM((2, nch, LANES), F32),
            pltpu.VMEM((2, nch, LANES), F32),
        ],
        compiler_params=_params("arbitrary", "arbitrary"),
        name="rglru",
    )(p, p, conv_w, conv_b, w4, b4, lam, keep, h0)


def _lru_chain_inputs(state_l):
    c = jnp.arange(N_CHUNKS)
    ctx_per_seq = SEQ // CHUNK
    lat_per_seq = DEC_SEQ // CHUNK
    is_ctx = c < CTX_CHUNKS
    idx = jnp.where(is_ctx, c % ctx_per_seq, (c - CTX_CHUNKS) % lat_per_seq)
    per = jnp.where(is_ctx, ctx_per_seq, lat_per_seq)
    start = jnp.stack([idx == 0, idx == per - 1])
    keep = jnp.broadcast_to(jnp.where(start, 0.0, 1.0)[:, :, None], (2, N_CHUNKS, LANES)).astype(F32)
    lat_b = jnp.clip((c - CTX_CHUNKS) // lat_per_seq, 0, DEC_BATCH - 1)
    seed = jnp.transpose(state_l, (1, 0, 2))[:, lat_b, :]
    h0 = jnp.where((start & ~is_ctx[None, :])[:, :, None], seed, 0.0).astype(F32)
    return keep, h0


def _pool_kernel(x_ref, w_ref, sc_ref, o_ref):
    tile = pl.program_id(0)
    ext_rows = POOL_CHUNK + 2 * POOL_HALO
    is_ctx = tile * POOL_TILE < N_CTX
    erow = lax.broadcasted_iota(jnp.int32, (ext_rows, LANES), 0)
    row = lax.broadcasted_iota(jnp.int32, (POOL_CHUNK, LANES), 0)

    def shift_up(x, k):
        return x if k == 0 else pltpu.roll(x, ext_rows - k, 0)

    for g, win in enumerate(POOL_WINDOWS):
        half = win // 2
        cols = slice(g * POOL_GROUP, (g + 1) * POOL_GROUP)
        w = w_ref[g]
        sc = sc_ref[:, cols]

        def body(c, carry):
            r0 = pl.multiple_of(c * POOL_CHUNK, POOL_CHUNK)
            seq_len = jnp.where(is_ctx, SEQ, DEC_SEQ)
            t0 = jnp.where(is_ctx, 0, (tile * POOL_TILE + r0) & (DEC_SEQ - 1))
            first = t0 == 0
            last = t0 + POOL_CHUNK == seq_len
            lo = pl.multiple_of(jnp.maximum(r0 - POOL_HALO, 0), POOL_HALO)
            hi = pl.multiple_of(jnp.minimum(r0 + POOL_CHUNK, POOL_TILE - POOL_HALO), POOL_HALO)
            cur = x_ref[pl.ds(r0, POOL_CHUNK), cols]
            ext = jnp.concatenate(
                [x_ref[pl.ds(lo, POOL_HALO), cols], cur, x_ref[pl.ds(hi, POOL_HALO), cols]], axis=0)
            outside = jnp.logical_or(jnp.logical_and(first, erow < POOL_HALO),
                                     jnp.logical_and(last, erow >= POOL_HALO + POOL_CHUNK))
            acc = jnp.where(outside, 0.0, ext)
            span = 1
            while span < win:
                acc = acc + shift_up(acc, span)
                span *= 2
            total = shift_up(acc, POOL_HALO - half)[:POOL_CHUNK]
            t = t0 + row
            cnt = jnp.minimum(t + half, seq_len) - jnp.maximum(t - half, 0)
            pooled = total / cnt.astype(F32) - cur
            y = _dot(pooled.astype(BF16), w) * sc
            o_ref[pl.ds(r0, POOL_CHUNK), cols] = y.astype(o_ref.dtype)
            return carry

        lax.fori_loop(0, POOL_TILE // POOL_CHUNK, body, 0, unroll=2)


def _pool(p, pool_w_bf16, pool_scale):
    blk = COL_PU // D_POOL
    return pl.pallas_call(
        _pool_kernel,
        grid=(N_TOK // POOL_TILE,),
        in_specs=[
            pl.BlockSpec((POOL_TILE, D_POOL), lambda i: (i, blk)),
            pl.BlockSpec((len(POOL_WINDOWS), POOL_GROUP, POOL_GROUP), lambda i: (0, 0, 0)),
            pl.BlockSpec((1, D_POOL), lambda i: (0, 0)),
        ],
        out_specs=pl.BlockSpec((POOL_TILE, D_POOL), lambda i: (i, 0)),
        out_shape=jax.ShapeDtypeStruct((N_TOK, D_POOL), BF16),
        compiler_params=_params("arbitrary"),
        name="pool_mixer",
    )(p, pool_w_bf16, pool_scale)


OUTPROJ_TM = 512


def _outproj_kernel(attn_ref, lru_ref, pool_ref, *refs, two_x):
    if two_x:
        xa_ref, xb_ref, gate_ref, wa_ref, wl_ref, wp_ref, o_ref = refs
        x = jnp.where(pl.program_id(0) < N_CTX // OUTPROJ_TM, xa_ref[...], xb_ref[...])
    else:
        x_ref, gate_ref, wa_ref, wl_ref, wp_ref, o_ref = refs
        x = x_ref[...]
    mix = _dot(attn_ref[...], wa_ref[...]) + _dot(lru_ref[...], wl_ref[...]) + _dot(pool_ref[...], wp_ref[...])
    o_ref[...] = x + gate_ref[...] * mix


def _outproj(attn, lru, pool, xs, mod, w_out_bf16):
    tm = OUTPROJ_TM
    nt = N_TOK // tm
    n_ctx_tiles = N_CTX // tm
    two_x = len(xs) == 2
    if two_x:
        x_specs = [pl.BlockSpec((tm, D_MODEL), lambda i: (jnp.minimum(i, n_ctx_tiles - 1), 0)),
                   pl.BlockSpec((tm, D_MODEL), lambda i: (jnp.maximum(i - n_ctx_tiles, 0), 0))]
    else:
        x_specs = [pl.BlockSpec((tm, D_MODEL), lambda i: (i, 0))]
    return pl.pallas_call(
        functools.partial(_outproj_kernel, two_x=two_x),
        grid=(nt,),
        in_specs=[
            pl.BlockSpec((tm, D_ATTN), lambda i: (i, 0)),
            pl.BlockSpec((tm, D_LRU), lambda i: (i, 0)),
            pl.BlockSpec((tm, D_POOL), lambda i: (i, 0)),
        ] + x_specs + [
            pl.BlockSpec((None, 1, D_MODEL), lambda i: (_group_of_tile(i, tm) * N_MOD + 2, 0, 0)),
            pl.BlockSpec((D_ATTN, D_MODEL), lambda i: (0, 0)),
            pl.BlockSpec((D_LRU, D_MODEL), lambda i: (D_ATTN // D_LRU, 0)),
            pl.BlockSpec((D_POOL, D_MODEL), lambda i: ((D_ATTN + D_LRU) // D_POOL, 0)),
        ],
        out_specs=pl.BlockSpec((tm, D_MODEL), lambda i: (i, 0)),
        out_shape=jax.ShapeDtypeStruct((N_TOK, D_MODEL), F32),
        compiler_params=_params("arbitrary"),
        name="outproj",
    )(attn, lru, pool, *xs, mod, w_out_bf16, w_out_bf16, w_out_bf16)


FFN_TM = 1024
FFN_TF = 512
FFN_TN = 512
FFN_CAST_COLS = 2 * FFN_TF
FFN_PIECE = 128
FFN_PIECES = FFN_TM // FFN_PIECE


def _ffn_kernel(xn_ref, xc_ref, g_ref, shift_ref, scale_ref, gate_ref, wg_ref, wu_ref, w2_ref, fg_ref, *rest,
                final_norm, cast_next):
    if cast_next:
        w1n_f32_ref, w2n_f32_ref, o_ref, w1n_ref, w2n_ref, h0_ref, h1_ref = rest
    else:
        o_ref, h0_ref, h1_ref = rest
    i, j = pl.program_id(0), pl.program_id(1)
    odd = i % 2 == 1
    row0 = pl.multiple_of(jnp.minimum(j, FFN_PIECES - 1) * FFN_PIECE, FFN_PIECE)
    gain = g_ref[...] * (1.0 + scale_ref[...])
    shift = shift_ref[...]
    once = jnp.where(j < FFN_PIECES, 1.0, 0.0)

    def norm_piece(h_write):
        for c in range(FFN_PIECE // NORM_ROWS):
            x = xn_ref[c * NORM_ROWS:(c + 1) * NORM_ROWS, :]
            ms = jnp.mean(x * x, axis=-1, keepdims=True)
            rows = pl.ds(pl.multiple_of(row0 + c * NORM_ROWS, NORM_ROWS), NORM_ROWS)
            h_write[rows, :] = (x * lax.rsqrt(ms + RMS_EPS) * gain + shift).astype(BF16)

    def step(h_write, h_read):
        norm_piece(h_write)
        h = h_read[...]
        act = (_silu(_dot(h, wg_ref[...])) * _dot(h, wu_ref[...])).astype(BF16)
        for n in range(D_MODEL // FFN_TN):
            cols = slice(n * FFN_TN, (n + 1) * FFN_TN)
            o_ref[:, cols] += gate_ref[:, cols] * _dot(act, w2_ref[:, cols])
        o_ref[pl.ds(row0, FFN_PIECE), :] += once * xc_ref[...]

    @pl.when(i == 0)
    def _():
        norm_piece(h0_ref)

    @pl.when(jnp.logical_and(i > 0, j == 0))
    def _():
        o_ref[...] = jnp.zeros_like(o_ref)

    @pl.when(odd)
    def _():
        step(h1_ref, h0_ref)

    @pl.when(jnp.logical_and(i > 0, jnp.logical_not(odd)))
    def _():
        step(h0_ref, h1_ref)

    if cast_next:
        @pl.when(jnp.logical_and(i >= 1, i <= CAST_TILES))
        def _():
            w1n_ref[...] = w1n_f32_ref[...].astype(BF16)
            w2n_ref[...] = w2n_f32_ref[...].astype(BF16)

    if final_norm:
        @pl.when(jnp.logical_and(i > 0, j == pl.num_programs(1) - 1))
        def _():
            fgain = fg_ref[...]

            def body(r, carry):
                r0 = pl.multiple_of(r * NORM_ROWS, NORM_ROWS)
                x = o_ref[pl.ds(r0, NORM_ROWS), :]
                ms = jnp.mean(x * x, axis=-1, keepdims=True)
                o_ref[pl.ds(r0, NORM_ROWS), :] = x * lax.rsqrt(ms + RMS_EPS) * fgain
                return carry

            lax.fori_loop(0, FFN_TM // NORM_ROWS, body, 0, unroll=NORM_UNROLL)


def _ffn(x, norm_g, mod, w1_l, w2_l, final_g, layer, tile0, ntiles, final_norm, w1_f32=None, w2_f32=None):
    tm, tf = FFN_TM, FFN_TF
    nf = D_FF // tf
    cast_next = w1_f32 is not None
    assert FFN_PIECES <= nf

    def norm_tile(i):
        return tile0 + jnp.minimum(i, ntiles - 1)

    def mm_tile(i):
        return tile0 + jnp.maximum(i - 1, 0)

    def piece(j):
        return jnp.minimum(j, FFN_PIECES - 1)

    def wcol(i, j):
        return jnp.where(i == 0, 0, j)

    def mod_spec(which, tile_fn):
        return pl.BlockSpec((None, 1, D_MODEL), lambda i, j: (_group_of_tile(tile_fn(i), tm) * N_MOD + which, 0, 0))

    in_specs = [
        pl.BlockSpec((FFN_PIECE, D_MODEL), lambda i, j: (norm_tile(i) * FFN_PIECES + piece(j), 0)),
        pl.BlockSpec((FFN_PIECE, D_MODEL), lambda i, j: (mm_tile(i) * FFN_PIECES + piece(j), 0)),
        pl.BlockSpec((1, D_MODEL), lambda i, j: (0, 0)),
        mod_spec(3, norm_tile),
        mod_spec(4, norm_tile),
        mod_spec(5, mm_tile),
        pl.BlockSpec((D_MODEL, tf), lambda i, j: (0, wcol(i, j))),
        pl.BlockSpec((D_MODEL, tf), lambda i, j: (0, nf + wcol(i, j))),
        pl.BlockSpec((tf, D_MODEL), lambda i, j: (wcol(i, j), 0)),
        pl.BlockSpec((1, D_MODEL), lambda i, j: (0, 0)),
    ]
    out_specs = [pl.BlockSpec((tm, D_MODEL), lambda i, j: (jnp.maximum(i - 1, 0), 0))]
    out_shape = [jax.ShapeDtypeStruct((ntiles * tm, D_MODEL), F32)]
    args = [x, x, norm_g, mod, mod, mod, w1_l, w1_l, w2_l, final_g]
    if cast_next:
        assert ntiles >= CAST_TILES and 2 * D_FF == nf * FFN_CAST_COLS

        def w1_idx(i, j):
            t = i - 1
            return jnp.clip(t, 0, CAST_TILES - 1), jnp.where(t < 0, 0, jnp.where(t < CAST_TILES, j, nf - 1))

        def w2_idx(i, j):
            return w1_idx(i, j)[::-1]

        nxt = layer + 1
        in_specs += [
            pl.BlockSpec((None, CAST_ROWS, FFN_CAST_COLS), lambda i, j: (nxt,) + w1_idx(i, j)),
            pl.BlockSpec((None, tf, CAST_ROWS), lambda i, j: (nxt,) + w2_idx(i, j)),
        ]
        out_specs += [pl.BlockSpec((CAST_ROWS, FFN_CAST_COLS), w1_idx), pl.BlockSpec((tf, CAST_ROWS), w2_idx)]
        out_shape += [jax.ShapeDtypeStruct((D_MODEL, 2 * D_FF), BF16), jax.ShapeDtypeStruct((D_FF, D_MODEL), BF16)]
        args += [w1_f32, w2_f32]
    out = pl.pallas_call(
        functools.partial(_ffn_kernel, final_norm=final_norm, cast_next=cast_next),
        grid=(ntiles + 1, nf),
        in_specs=in_specs,
        out_specs=out_specs,
        out_shape=out_shape,
        scratch_shapes=[pltpu.VMEM((tm, D_MODEL), BF16), pltpu.VMEM((tm, D_MODEL), BF16)],
        compiler_params=_params("arbitrary", "arbitrary"),
        name="ffn_final" if final_norm else ("ffn_cast" if cast_next else "ffn"),
    )(*args)
    return out if cast_next else out[0]


def kernel(x_prompt, x_sample, cache_k, cache_v, state_lru, c, c_ctx, mod_w, mod_b, norm_mix, norm_ffn,
           w_in, attn_sink, conv_w, conv_b, lru_wa, lru_ba, lru_wx, lru_bx, lru_lambda, pool_w,
           pool_scale, w_out, ffn_w1, ffn_w2, norm_final):
    xs = (x_prompt.reshape(N_CTX, D_MODEL), x_sample.reshape(N_LAT, D_MODEL))

    cond = jnp.zeros((COND_ROWS, D_MODEL), F32).at[0].set(c_ctx).at[1:1 + DEC_BATCH].set(c)
    mod_all = _mod_vectors(cond, mod_w, mod_b)
    mod_all = mod_all[:, :N_GROUPS].reshape(DEPTH, N_GROUPS * N_MOD, 1, D_MODEL)

    w_in_l = w_in[0].astype(BF16)
    w_out_l = w_out[0].astype(BF16)
    w1_l = ffn_w1[0].astype(BF16)
    w2_l = ffn_w2[0].astype(BF16)
    pool_w_b = pool_w.astype(BF16)
    w4 = (0.5 * jnp.concatenate([lru_wa[:, 0], lru_wx[:, 0], lru_wa[:, 1], lru_wx[:, 1]], axis=-1)).astype(BF16)
    ba = lru_ba.reshape(DEPTH, 2, LRU_BLOCKS, LRU_BLK)
    bx = lru_bx.reshape(DEPTH, 2, LRU_BLOCKS, LRU_BLK)
    b4 = 0.5 * jnp.stack([ba[:, 0], bx[:, 0], ba[:, 1], bx[:, 1]], axis=2).reshape(DEPTH, LRU_BLOCKS, 1, 4 * LRU_BLK)
    cache_k2 = cache_k.reshape(DEC_BATCH, DEPTH, PAST_LEN, D_KV)
    cache_v2 = cache_v.reshape(DEC_BATCH, DEPTH, PAST_LEN, D_KV)
    rope_tabs = _rope_tables()

    kc = jnp.zeros((BATCH, DEPTH, SEQ, D_KV), F32)
    vc = jnp.zeros((BATCH, DEPTH, SEQ, D_KV), F32)
    attn = jnp.zeros((N_TOK, D_ATTN), BF16)
    hs_new = []
    for l in range(DEPTH):
        mod = mod_all[l]
        last = l == DEPTH - 1
        w_out_cur = w_out_l
        if last:
            p, kc, vc = _inproj(xs, norm_mix[l][None, :], mod, w_in_l, kc, vc, l)
        else:
            p, kc, vc, w_in_l, w_out_l = _inproj(xs, norm_mix[l][None, :], mod, w_in_l, kc, vc, l, w_in, w_out)

        attn = _ctx_attention(p, attn_sink[l], attn)
        attn = _lat_attention(p, attn_sink[l], cache_k2, cache_v2, rope_tabs, attn, l)
        keep, h0 = _lru_chain_inputs(state_lru[:, l])
        lru, hend = _lru(p, conv_w[l], conv_b[l][None, :], w4[l], b4[l], lru_lambda[l], keep, h0)
        pool = _pool(p, pool_w_b[l], pool_scale[l][None, :])

        ctx_per_seq = SEQ // CHUNK
        fwd_fin = hend[0, ctx_per_seq - 1:CTX_CHUNKS:ctx_per_seq]
        bwd_fin = hend[1, 0:CTX_CHUNKS:ctx_per_seq]
        hs_new.append(jnp.stack([fwd_fin, bwd_fin], axis=1))

        x = _outproj(attn, lru, pool, xs, mod, w_out_cur)
        ffn_args = (x, norm_ffn[l][None, :], mod, w1_l, w2_l, norm_final[None, :], l)
        if not last:
            x, w1_l, w2_l = _ffn(*ffn_args, 0, N_TOK // FFN_TM, False, ffn_w1, ffn_w2)
            xs = (x,)

    y_prompt = _ffn(*ffn_args, 0, N_CTX // FFN_TM, True).reshape(BATCH, SEQ, D_MODEL)
    y_sample = _ffn(*ffn_args, N_CTX // FFN_TM, N_LAT // FFN_TM, True).reshape(DEC_BATCH, DEC_SEQ, D_MODEL)
    cache_shape = (BATCH, DEPTH, SEQ, N_KV_HEADS, HEAD_DIM)
    return (y_prompt, y_sample, kc.reshape(cache_shape), vc.reshape(cache_shape), jnp.stack(hs_new, axis=1))
```

```python
import functools
import math

import jax
import jax.numpy as jnp
from jax import lax
from jax.experimental import pallas as pl
from jax.experimental.pallas import tpu as pltpu

F32 = jnp.float32
BF16 = jnp.bfloat16

D_MODEL = 2048
BATCH = 32
SEQ = 256
DEPTH = 4
DEC_BATCH = 2
DEC_SEQ = 2048
PAST_LEN = 512
GRID_W = 64
HEAD_DIM = 128
N_HEADS = 8
N_KV_HEADS = 2
KV_GROUP = N_HEADS // N_KV_HEADS
D_ATTN = N_HEADS * HEAD_DIM
D_KV = N_KV_HEADS * HEAD_DIM
WINDOW = 128
Q_BLOCK = 128
ROPE_THETA = 10000.0
D_LRU = D_MODEL // 4
LRU_BLOCKS = 4
LRU_BLK = D_LRU // LRU_BLOCKS
CONV_W = 4
LRU_C = 8.0
D_POOL = D_MODEL // 4
POOL_WINDOWS = (2, 4, 8, 16)
POOL_GROUP = D_POOL // len(POOL_WINDOWS)
D_IN = D_ATTN + 2 * D_KV + 2 * D_LRU + D_POOL
D_FF = 5632
N_MOD = 6
RMS_EPS = 1e-6
NEG_INF = -1e30

N_CTX = BATCH * SEQ
N_LAT = DEC_BATCH * DEC_SEQ
N_TOK = N_CTX + N_LAT
N_GROUPS = 1 + DEC_BATCH
COND_ROWS = 16

COL_Q = 0
COL_K = D_ATTN
COL_V = D_ATTN + D_KV
COL_LX = D_ATTN + 2 * D_KV
COL_LG = COL_LX + D_LRU
COL_PU = COL_LG + D_LRU

VMEM_LIMIT = 56 * 1024 * 1024
MATMUL_STAGING_BYTES = 6 * 1024 * 1024

NORM_ROWS = 32
NORM_UNROLL = 4
LANES = 128
SUBLANES = 8

CHUNK = 128
CHUNK_PITCH = 136
SQRT_TINY = 1e-30
SCAN_UNROLL = 4
LRU_TILE_CHUNKS = 48
LRU_TILE = LRU_TILE_CHUNKS * CHUNK
N_CHUNKS = N_TOK // CHUNK
CTX_CHUNKS = N_CTX // CHUNK

POOL_TILE = DEC_SEQ
POOL_CHUNK = 256
POOL_HALO = 8


def _dot(a, b):
    return jnp.dot(a, b, preferred_element_type=F32)


def _dot_nt(a, b):
    return lax.dot_general(a, b, (((1,), (1,)), ((), ())), preferred_element_type=F32)


def _sigmoid(x):
    return 1.0 / (1.0 + jnp.exp(-x))


def _silu(x):
    return x * _sigmoid(x)


def _gelu_tanh(x):
    c = math.sqrt(2.0 / math.pi)
    return 0.5 * x * (1.0 + jnp.tanh(c * (x + 0.044715 * (x * x * x))))


def _params(*sem, vmem_limit=VMEM_LIMIT):
    return pltpu.CompilerParams(dimension_semantics=sem, vmem_limit_bytes=vmem_limit)


CAST_TILES = 8
CAST_ROWS = D_MODEL // CAST_TILES


def _group_of_tile(i, tm):
    n_ctx_tiles = N_CTX // tm
    tiles_per_lat = DEC_SEQ // tm
    return jnp.where(i < n_ctx_tiles, 0, 1 + (i - n_ctx_tiles) // tiles_per_lat)


MOD_TN = 1024


def _mod_kernel(cond_ref, w_ref, b_ref, o_ref):
    s = _silu(cond_ref[...]).astype(BF16)
    o_ref[...] = _dot(s, w_ref[...].astype(BF16)) + b_ref[...]


def _mod_vectors(cond, mod_w, mod_b):
    n = N_MOD * D_MODEL
    return pl.pallas_call(
        _mod_kernel,
        grid=(DEPTH, n // MOD_TN),
        in_specs=[
            pl.BlockSpec((COND_ROWS, D_MODEL), lambda l, j: (0, 0)),
            pl.BlockSpec((None, D_MODEL, MOD_TN), lambda l, j: (l, 0, j)),
            pl.BlockSpec((None, 1, MOD_TN), lambda l, j: (l, 0, j)),
        ],
        out_specs=pl.BlockSpec((None, COND_ROWS, MOD_TN), lambda l, j: (l, 0, j)),
        out_shape=jax.ShapeDtypeStruct((DEPTH, COND_ROWS, n), F32),
        compiler_params=_params("arbitrary", "arbitrary"),
        name="mod_vectors",
    )(cond, mod_w, mod_b.reshape(DEPTH, 1, n))


INPROJ_TM = 512
INPROJ_TILES = N_TOK // INPROJ_TM
INPROJ_CAST_COLS = 1024


def _inproj_kernel(*refs, cast_next, two_x):
    refs = list(refs)
    xa_ref = refs.pop(0)
    xb_ref = refs.pop(0) if two_x else None
    g_ref, shift_ref, scale_ref, w_ref, kc_in_ref, vc_in_ref = refs[:6]
    del kc_in_ref, vc_in_ref
    refs = refs[6:]
    if cast_next:
        win_f32_ref, wout_f32_ref, o_ref, kc_ref, vc_ref, win_ref, wout_ref, h0_ref, h1_ref = refs
    else:
        o_ref, kc_ref, vc_ref, h0_ref, h1_ref = refs
    s = pl.program_id(0)
    odd = s % 2 == 1

    gain = g_ref[...] * (1.0 + scale_ref[...])
    shift = shift_ref[...]
    from_a = s < N_CTX // INPROJ_TM

    def norm_into(h_write):
        for c in range(INPROJ_TM // NORM_ROWS):
            rows = slice(c * NORM_ROWS, (c + 1) * NORM_ROWS)
            x = xa_ref[rows, :]
            if two_x:
                x = jnp.where(from_a, x, xb_ref[rows, :])
            ms = jnp.mean(x * x, axis=-1, keepdims=True)
            h_write[rows, :] = (x * lax.rsqrt(ms + RMS_EPS) * gain + shift).astype(BF16)

    def convert_next():
        if cast_next:
            win_ref[...] = win_f32_ref[...].astype(BF16)
            wout_ref[...] = wout_f32_ref[...].astype(BF16)

    def step(h_write, h_read):
        norm_into(h_write)
        convert_next()
        o_ref[...] = _dot(h_read[...], w_ref[...])

    @pl.when(s == 0)
    def _():
        norm_into(h0_ref)
        convert_next()

    @pl.when(jnp.logical_and(s > 0, jnp.logical_not(odd)))
    def _():
        step(h0_ref, h1_ref)

    @pl.when(odd)
    def _():
        step(h1_ref, h0_ref)

    @pl.when(jnp.logical_and(s >= 1, s <= N_CTX // INPROJ_TM))
    def _():
        kc_ref[...] = o_ref[:, COL_K:COL_K + D_KV].reshape(kc_ref.shape)
        vc_ref[...] = o_ref[:, COL_V:COL_V + D_KV].reshape(vc_ref.shape)


def _inproj(xs, norm_g, mod, w_in_l, kc, vc, layer, w_in_f32=None, w_out_f32=None):
    tm = INPROJ_TM
    nt = INPROJ_TILES
    cast_next = w_in_f32 is not None
    two_x = len(xs) == 2
    seqs = tm // SEQ
    n_ctx_tiles = N_CTX // tm

    def norm_tile(s):
        return jnp.minimum(s, nt - 1)

    def mm_tile(s):
        return jnp.maximum(s - 1, 0)

    def mod_spec(which):
        return pl.BlockSpec((None, 1, D_MODEL), lambda s: (_group_of_tile(norm_tile(s), tm) * N_MOD + which, 0, 0))

    def cast_idx(ncols):
        def idx(s):
            t = jnp.minimum(s, CAST_TILES * ncols - 1)
            return t // ncols, t % ncols
        return idx

    cache_spec = pl.BlockSpec((seqs, None, SEQ, D_KV),
                              lambda s: (jnp.minimum(mm_tile(s), n_ctx_tiles - 1), layer, 0, 0))
    cache_shape = jax.ShapeDtypeStruct((BATCH, DEPTH, SEQ, D_KV), F32)
    if two_x:
        x_specs = [pl.BlockSpec((tm, D_MODEL), lambda s: (jnp.minimum(s, n_ctx_tiles - 1), 0)),
                   pl.BlockSpec((tm, D_MODEL), lambda s: (jnp.clip(s - n_ctx_tiles, 0, nt - n_ctx_tiles - 1), 0))]
    else:
        x_specs = [pl.BlockSpec((tm, D_MODEL), lambda s: (norm_tile(s), 0))]
    in_specs = x_specs + [
        pl.BlockSpec((1, D_MODEL), lambda s: (0, 0)),
        mod_spec(0),
        mod_spec(1),
        pl.BlockSpec((D_MODEL, D_IN), lambda s: (0, 0), pipeline_mode=pl.Buffered(1)),
        pl.BlockSpec(memory_space=pl.ANY),
        pl.BlockSpec(memory_space=pl.ANY),
    ]
    out_specs = [pl.BlockSpec((tm, D_IN), lambda s: (mm_tile(s), 0)), cache_spec, cache_spec]
    out_shape = [jax.ShapeDtypeStruct((N_TOK, D_IN), F32), cache_shape, cache_shape]
    args = list(xs) + [norm_g, mod, mod, w_in_l, kc, vc]
    n_in = len(args)
    if cast_next:
        nxt = layer + 1
        cc = INPROJ_CAST_COLS
        win_idx = cast_idx(D_IN // cc)
        wout_idx = cast_idx(D_MODEL // cc)
        in_specs += [
            pl.BlockSpec((None, CAST_ROWS, cc), lambda s: (nxt,) + win_idx(s)),
            pl.BlockSpec((None, CAST_ROWS, cc), lambda s: (nxt,) + wout_idx(s)),
        ]
        out_specs += [pl.BlockSpec((CAST_ROWS, cc), win_idx), pl.BlockSpec((CAST_ROWS, cc), wout_idx)]
        out_shape += [jax.ShapeDtypeStruct((D_MODEL, D_IN), BF16), jax.ShapeDtypeStruct((D_MODEL, D_MODEL), BF16)]
        args += [w_in_f32, w_out_f32]
    cast_bytes = 2 * 2 * CAST_ROWS * INPROJ_CAST_COLS * (4 + 2) if cast_next else 0
    vmem_bytes = (D_MODEL * D_IN * 2 + 2 * len(xs) * tm * D_MODEL * 4 + 2 * tm * D_IN * 4 + 2 * tm * D_MODEL * 2
                  + 2 * 2 * tm * D_KV * 4 + cast_bytes + MATMUL_STAGING_BYTES)
    return pl.pallas_call(
        functools.partial(_inproj_kernel, cast_next=cast_next, two_x=two_x),
        grid=(nt + 1,),
        in_specs=in_specs,
        out_specs=out_specs,
        out_shape=out_shape,
        input_output_aliases={n_in - 2: 1, n_in - 1: 2},
        scratch_shapes=[pltpu.VMEM((tm, D_MODEL), BF16), pltpu.VMEM((tm, D_MODEL), BF16)],
        compiler_params=_params("arbitrary", vmem_limit=vmem_bytes),
        name="inproj_cast" if cast_next else "inproj",
    )(*args)


def _sink_column(sink_ref, kv_head, rows_per_head):
    n = KV_GROUP * rows_per_head
    row = lax.broadcasted_iota(jnp.int32, (n, 1), 0)
    col = jnp.full((n, 1), sink_ref[kv_head * KV_GROUP + KV_GROUP - 1], F32)
    for hh in range(KV_GROUP - 2, -1, -1):
        col = jnp.where(row < (hh + 1) * rows_per_head, sink_ref[kv_head * KV_GROUP + hh], col)
    return col


def _stack_heads(q, kv_head):
    return jnp.concatenate(
        [q[:, (kv_head * KV_GROUP + hh) * HEAD_DIM:(kv_head * KV_GROUP + hh + 1) * HEAD_DIM]
         for hh in range(KV_GROUP)], axis=0)


def _ctx_attn_kernel(sink_ref, q_ref, kv_ref, buf_hbm_ref, *rest):
    del buf_hbm_ref
    if len(rest) == 3:
        w_f32_ref, o_ref, w_ref = rest
        w_ref[...] = w_f32_ref[...].astype(BF16)
    else:
        (o_ref,) = rest
    scale = HEAD_DIM ** -0.5
    q = q_ref[...]
    kv = kv_ref[...]
    ones = jnp.ones((SEQ, HEAD_DIM), BF16)
    for g in range(N_KV_HEADS):
        k = kv[:, g * HEAD_DIM:(g + 1) * HEAD_DIM].astype(BF16)
        v = kv[:, D_KV + g * HEAD_DIM:D_KV + (g + 1) * HEAD_DIM].astype(BF16)
        qs = (_stack_heads(q, g) * scale).astype(BF16)
        s = _dot_nt(qs, k)
        sink_col = _sink_column(sink_ref, g, SEQ)
        m = jnp.maximum(jnp.max(s, axis=-1, keepdims=True), sink_col)
        p = jnp.exp(s - m).astype(BF16)
        pv = _dot(p, jnp.concatenate([v, ones], axis=1))
        o = pv[:, :HEAD_DIM] / (pv[:, HEAD_DIM:] + jnp.exp(sink_col - m))
        for hh in range(KV_GROUP):
            h = g * KV_GROUP + hh
            o_ref[:, h * HEAD_DIM:(h + 1) * HEAD_DIM] = o[hh * SEQ:(hh + 1) * SEQ].astype(o_ref.dtype)


def _ctx_attention(p, sink, buf, w_f32=None):
    kv_blk = 2 * D_KV
    in_specs = [
        pl.BlockSpec((SEQ, D_ATTN), lambda b, s: (b, 0)),
        pl.BlockSpec((SEQ, kv_blk), lambda b, s: (b, COL_K // kv_blk)),
        pl.BlockSpec(memory_space=pl.ANY),
    ]
    out_specs = [pl.BlockSpec((SEQ, D_ATTN), lambda b, s: (b, 0))]
    out_shape = [jax.ShapeDtypeStruct((N_TOK, D_ATTN), BF16)]
    args = [sink, p, p, buf]
    if w_f32 is not None:
        rows = D_FF // BATCH
        assert rows * BATCH == D_FF and rows % 16 == 0
        in_specs.append(pl.BlockSpec((None, rows, D_MODEL), lambda b, s: (0, b, 0)))
        out_specs.append(pl.BlockSpec((rows, D_MODEL), lambda b, s: (b, 0)))
        out_shape.append(jax.ShapeDtypeStruct((D_FF, D_MODEL), BF16))
        args.append(w_f32)
    out = pl.pallas_call(
        _ctx_attn_kernel,
        grid_spec=pltpu.PrefetchScalarGridSpec(
            num_scalar_prefetch=1, grid=(BATCH,), in_specs=in_specs, out_specs=out_specs),
        out_shape=out_shape,
        input_output_aliases={3: 0},
        compiler_params=_params("arbitrary"),
        name="ctx_attention",
    )(*args)
    return out if w_f32 is not None else out[0]


def _rope(x, cos, sin_lo, sin_hi):
    return x * cos + pltpu.roll(x, 96, 1) * sin_lo + pltpu.roll(x, 32, 1) * sin_hi


def _lat_attn_kernel(sink_ref, q_ref, kv_ref, ck_ref, cv_ref, cos_ref, slo_ref, shi_ref, attn_hbm_ref, *rest):
    del attn_hbm_ref
    if len(rest) == 7:
        w_f32_ref, o_ref, w_ref, kt_s, v_s, ckt_s, cv_s = rest
        w_ref[...] = w_f32_ref[...].astype(BF16)
    else:
        o_ref, kt_s, v_s, ckt_s, cv_s = rest
    i = pl.program_id(1)
    scale = HEAD_DIM ** -0.5
    band = Q_BLOCK + 2 * WINDOW
    band_blocks = band // Q_BLOCK

    @pl.when(i == 0)
    def _():
        for g in range(N_KV_HEADS):
            sl = slice(g * HEAD_DIM, (g + 1) * HEAD_DIM)
            k = _rope(kv_ref[:, sl], cos_ref[...], slo_ref[...], shi_ref[...])
            for t in range(DEC_SEQ // Q_BLOCK):
                kt_s[g, t] = k[t * Q_BLOCK:(t + 1) * Q_BLOCK].T.astype(BF16)
            ckt_s[g] = ck_ref[:, sl].T.astype(BF16)
        v_s[...] = kv_ref[:, D_KV:2 * D_KV].astype(BF16)
        cv_s[...] = cv_ref[...].astype(BF16)

    q0 = pl.multiple_of(i * Q_BLOCK, Q_BLOCK)
    kb0 = jnp.clip(i - WINDOW // Q_BLOCK, 0, DEC_SEQ // Q_BLOCK - band_blocks)
    k0 = pl.multiple_of(kb0 * Q_BLOCK, Q_BLOCK)
    cos = cos_ref[pl.ds(q0, Q_BLOCK), :] * scale
    slo = slo_ref[pl.ds(q0, Q_BLOCK), :] * scale
    shi = shi_ref[pl.ds(q0, Q_BLOCK), :] * scale
    cos4 = jnp.concatenate([cos] * KV_GROUP, axis=0)
    slo4 = jnp.concatenate([slo] * KV_GROUP, axis=0)
    shi4 = jnp.concatenate([shi] * KV_GROUP, axis=0)

    qpos = q0 + lax.broadcasted_iota(jnp.int32, (Q_BLOCK, band), 0)
    kpos = k0 + lax.broadcasted_iota(jnp.int32, (Q_BLOCK, band), 1)
    bias1 = jnp.where(jnp.abs(qpos - kpos) <= WINDOW, 0.0, NEG_INF).astype(F32)
    bias = jnp.concatenate([bias1] * KV_GROUP, axis=0)
    ones = jnp.ones((band + PAST_LEN, HEAD_DIM), BF16)

    q = q_ref[...]
    for g in range(N_KV_HEADS):
        sl = slice(g * HEAD_DIM, (g + 1) * HEAD_DIM)
        qs = _rope(_stack_heads(q, g), cos4, slo4, shi4).astype(BF16)
        kt = jnp.concatenate([kt_s[g, kb0 + t] for t in range(band_blocks)] + [ckt_s[g]], axis=1)
        s = _dot(qs, kt)
        s_band = s[:, :band] + bias
        s_ctx = s[:, band:]
        sink_col = _sink_column(sink_ref, g, Q_BLOCK)
        m = jnp.maximum(jnp.maximum(jnp.max(s_band, axis=-1, keepdims=True),
                                    jnp.max(s_ctx, axis=-1, keepdims=True)), sink_col)
        p = jnp.concatenate([jnp.exp(s_band - m), jnp.exp(s_ctx - m)], axis=1).astype(BF16)
        v = jnp.concatenate([v_s[pl.ds(k0, band), sl], cv_s[:, sl]], axis=0)
        pv = _dot(p, jnp.concatenate([v, ones], axis=1))
        o = pv[:, :HEAD_DIM] / (pv[:, HEAD_DIM:] + jnp.exp(sink_col - m))
        for hh in range(KV_GROUP):
            h = g * KV_GROUP + hh
            o_ref[:, h * HEAD_DIM:(h + 1) * HEAD_DIM] = (
                o[hh * Q_BLOCK:(hh + 1) * Q_BLOCK].astype(o_ref.dtype))


def _lat_attention(p, sink, cache_k, cache_v, rope_tabs, attn, layer, w_f32=None):
    kv_blk = 2 * D_KV
    nq = DEC_SEQ // Q_BLOCK
    ctx_rows = N_CTX // Q_BLOCK
    seq_blk = N_CTX // DEC_SEQ
    tab_spec = pl.BlockSpec((DEC_SEQ, HEAD_DIM), lambda b, i, s: (0, 0))
    cache_spec = pl.BlockSpec((None, None, PAST_LEN, D_KV), lambda b, i, s: (b, layer, 0, 0))
    in_specs = [
        pl.BlockSpec((Q_BLOCK, D_ATTN), lambda b, i, s: (ctx_rows + b * nq + i, 0)),
        pl.BlockSpec((DEC_SEQ, kv_blk), lambda b, i, s: (seq_blk + b, COL_K // kv_blk)),
        cache_spec, cache_spec, tab_spec, tab_spec, tab_spec,
        pl.BlockSpec(memory_space=pl.ANY),
    ]
    out_specs = [pl.BlockSpec((Q_BLOCK, D_ATTN), lambda b, i, s: (ctx_rows + b * nq + i, 0))]
    out_shape = [jax.ShapeDtypeStruct((N_TOK, D_ATTN), BF16)]
    args = [sink, p, p, cache_k, cache_v, *rope_tabs, attn]
    if w_f32 is not None:
        col_blocks = DEC_BATCH * nq // CAST_TILES
        cols = 2 * D_FF // col_blocks
        assert col_blocks * CAST_TILES == DEC_BATCH * nq and cols * col_blocks == 2 * D_FF and cols % LANES == 0

        def w_idx(b, i, s):
            step = b * nq + i
            return step // col_blocks, step % col_blocks

        in_specs.append(pl.BlockSpec((None, CAST_ROWS, cols), lambda b, i, s: (0,) + w_idx(b, i, s)))
        out_specs.append(pl.BlockSpec((CAST_ROWS, cols), w_idx))
        out_shape.append(jax.ShapeDtypeStruct((D_MODEL, 2 * D_FF), BF16))
        args.append(w_f32)
    out = pl.pallas_call(
        _lat_attn_kernel,
        grid_spec=pltpu.PrefetchScalarGridSpec(
            num_scalar_prefetch=1,
            grid=(DEC_BATCH, nq),
            in_specs=in_specs,
            out_specs=out_specs,
            scratch_shapes=[
                pltpu.VMEM((N_KV_HEADS, DEC_SEQ // Q_BLOCK, HEAD_DIM, Q_BLOCK), BF16),
                pltpu.VMEM((DEC_SEQ, D_KV), BF16),
                pltpu.VMEM((N_KV_HEADS, HEAD_DIM, PAST_LEN), BF16),
                pltpu.VMEM((PAST_LEN, D_KV), BF16),
            ],
        ),
        out_shape=out_shape,
        input_output_aliases={8: 0},
        compiler_params=_params("arbitrary", "arbitrary"),
        name="lat_attention",
    )(*args)
    return out if w_f32 is not None else out[0]


def _rope_tables():
    t = jnp.arange(DEC_SEQ)
    pos = jnp.stack([t // GRID_W, t % GRID_W], axis=-1).astype(F32)
    rd = HEAD_DIM // 4
    inv = ROPE_THETA ** (-jnp.arange(rd, dtype=F32) / rd)
    ang = jnp.broadcast_to(pos[:, :, None, None] * inv, (DEC_SEQ, 2, 2, rd)).reshape(DEC_SEQ, HEAD_DIM)
    cos, sin = jnp.cos(ang), jnp.sin(ang)
    first = (jnp.arange(HEAD_DIM) % (2 * rd)) < rd
    return cos, jnp.where(first, -sin, 0.0), jnp.where(first, 0.0, sin)


def _seq_position(chunk):
    ctx_per_seq = SEQ // CHUNK
    lat_per_seq = DEC_SEQ // CHUNK
    is_ctx = chunk < CTX_CHUNKS
    idx = jnp.where(is_ctx, chunk % ctx_per_seq, (chunk - CTX_CHUNKS) % lat_per_seq)
    per = jnp.where(is_ctx, ctx_per_seq, lat_per_seq)
    return idx == 0, idx == per - 1


def _lru_kernel(lx_ref, lg_ref, cw_ref, cb_ref, w4_ref, b4_ref, lam_ref, keep_ref, h0_ref,
                o_ref, hend_ref, a_s, b_s, tot_s, loc_s, hin_s):
    tile = pl.program_id(1)
    nch = LRU_TILE_CHUNKS
    row = lax.broadcasted_iota(jnp.int32, (CHUNK, LANES), 0)
    lam = lam_ref[...]
    half_c = (0.5 * LRU_C) * (jnp.minimum(lam, 0.0) - jnp.log(1.0 + jnp.exp(-jnp.abs(lam))))
    cw = cw_ref[...]
    cb = cb_ref[...]
    w4 = w4_ref[...]
    b4 = b4_ref[...]

    def gates(c, carry):
        r0 = pl.multiple_of(c * CHUNK, CHUNK)
        first, last = _seq_position(tile * nch + c)
        lo = pl.multiple_of(jnp.maximum(r0 - SUBLANES, 0), SUBLANES)
        hi = pl.multiple_of(jnp.minimum(r0 + CHUNK, LRU_TILE - SUBLANES), SUBLANES)
        ext = jnp.concatenate(
            [lx_ref[pl.ds(lo, SUBLANES), :], lx_ref[pl.ds(r0, CHUNK), :], lx_ref[pl.ds(hi, SUBLANES), :]],
            axis=0)
        xm2 = jnp.where(jnp.logical_and(first, row < 2), 0.0, ext[6:6 + CHUNK])
        xm1 = jnp.where(jnp.logical_and(first, row < 1), 0.0, ext[7:7 + CHUNK])
        x0 = ext[8:8 + CHUNK]
        xp1 = jnp.where(jnp.logical_and(last, row >= CHUNK - 1), 0.0, ext[9:9 + CHUNK])
        u = xm2 * cw[0:1] + xm1 * cw[1:2] + x0 * cw[2:3] + xp1 * cw[3:4] + cb
        t = jnp.tanh(_dot(u.astype(BF16), w4) + b4)
        u_half = 0.5 * u
        s0 = pl.multiple_of(c * CHUNK_PITCH, SUBLANES)
        for d in range(2):
            t_r = t[:, (2 * d) * LANES:(2 * d + 1) * LANES]
            t_i = t[:, (2 * d + 1) * LANES:(2 * d + 2) * LANES]
            a = jnp.exp(t_r * half_c[d:d + 1] + half_c[d:d + 1])
            a_s[d, pl.ds(s0, CHUNK), :] = a
            x = 1.0 - a * a
            root = x * lax.rsqrt(jnp.maximum(x, SQRT_TINY))
            b_s[d, pl.ds(s0, CHUNK), :] = root * (t_i + 1.0) * u_half
        return carry

    lax.fori_loop(0, nch, gates, 0, unroll=2)

    def time_index(k, d):
        return pl.ds(k if d == 0 else CHUNK - 1 - k, nch, stride=CHUNK_PITCH)

    def local_scan(k, carry):
        out = []
        for d in range(2):
            a_t = a_s[d, time_index(k, d), :]
            acc_a = carry[2 * d] * a_t
            h = a_t * carry[2 * d + 1] + b_s[d, time_index(k, d), :]
            a_s[d, time_index(k, d), :] = acc_a
            b_s[d, time_index(k, d), :] = h
            out += [acc_a, h]
        return tuple(out)

    ones = jnp.ones((nch, LANES), F32)
    zeros = jnp.zeros((nch, LANES), F32)
    fin = lax.fori_loop(0, CHUNK, local_scan, (ones, zeros, ones, zeros), unroll=SCAN_UNROLL)
    for d in range(2):
        tot_s[d] = fin[2 * d]
        loc_s[d] = fin[2 * d + 1]

    for d in range(2):
        prev = jnp.zeros((1, LANES), F32)
        order = range(nch) if d == 0 else range(nch - 1, -1, -1)
        for c in order:
            hin = keep_ref[d, c:c + 1, :] * prev + h0_ref[d, c:c + 1, :]
            hin_s[d, c:c + 1, :] = hin
            prev = tot_s[d, c:c + 1, :] * hin + loc_s[d, c:c + 1, :]
            hend_ref[d, c:c + 1, :] = prev

    def emit(c, carry):
        r0 = pl.multiple_of(c * CHUNK, CHUNK)
        s0 = pl.multiple_of(c * CHUNK_PITCH, SUBLANES)
        y = None
        for d in range(2):
            h = b_s[d, pl.ds(s0, CHUNK), :] + a_s[d, pl.ds(s0, CHUNK), :] * hin_s[d, pl.ds(c, 1), :]
            y = h if y is None else y + h
        o_ref[pl.ds(r0, CHUNK), :] = (y * _gelu_tanh(lg_ref[pl.ds(r0, CHUNK), :])).astype(o_ref.dtype)
        return carry

    lax.fori_loop(0, nch, emit, 0, unroll=2)


def _lru(p, conv_w, conv_b, w4, b4, lam, keep, h0):
    nt = N_TOK // LRU_TILE
    nch = LRU_TILE_CHUNKS
    lx_blk = COL_LX // LRU_BLK
    lg_blk = COL_LG // LRU_BLK
    return pl.pallas_call(
        _lru_kernel,
        grid=(LRU_BLOCKS, nt),
        in_specs=[
            pl.BlockSpec((LRU_TILE, LRU_BLK), lambda n, r: (r, lx_blk + n)),
            pl.BlockSpec((LRU_TILE, LRU_BLK), lambda n, r: (r, lg_blk + n)),
            pl.BlockSpec((CONV_W, LRU_BLK), lambda n, r: (0, n)),
            pl.BlockSpec((1, LRU_BLK), lambda n, r: (0, n)),
            pl.BlockSpec((None, LRU_BLK, 4 * LRU_BLK), lambda n, r: (n, 0, 0)),
            pl.BlockSpec((None, 1, 4 * LRU_BLK), lambda n, r: (n, 0, 0)),
            pl.BlockSpec((2, LRU_BLK), lambda n, r: (0, n)),
            pl.BlockSpec((2, nch, LRU_BLK), lambda n, r: (0, r, 0)),
            pl.BlockSpec((2, nch, LRU_BLK), lambda n, r: (0, r, n)),
        ],
        out_specs=[
            pl.BlockSpec((LRU_TILE, LRU_BLK), lambda n, r: (r, n)),
            pl.BlockSpec((2, nch, LRU_BLK), lambda n, r: (0, r, n)),
        ],
        out_shape=[
            jax.ShapeDtypeStruct((N_TOK, D_LRU), BF16),
            jax.ShapeDtypeStruct((2, N_CHUNKS, D_LRU), F32),
        ],
        scratch_shapes=[
            pltpu.VMEM((2, nch * CHUNK_PITCH, LANES), F32),
            pltpu.VMEM((2, nch * CHUNK_PITCH, LANES), F32),
            pltpu.VMEM((2, nch, LANES), F32),
            pltpu.VMEM((2, nch, LANES), F32),
            pltpu.VMEM((2, nch, LANES), F32),
        ],
        compiler_params=_params("arbitrary", "arbitrary"),
        name="rglru",
    )(p, p, conv_w, conv_b, w4, b4, lam, keep, h0)


def _lru_chain_inputs(state_l):
    c = jnp.arange(N_CHUNKS)
    ctx_per_seq = SEQ // CHUNK
    lat_per_seq = DEC_SEQ // CHUNK
    is_ctx = c < CTX_CHUNKS
    idx = jnp.where(is_ctx, c % ctx_per_seq, (c - CTX_CHUNKS) % lat_per_seq)
    per = jnp.where(is_ctx, ctx_per_seq, lat_per_seq)
    start = jnp.stack([idx == 0, idx == per - 1])
    keep = jnp.broadcast_to(jnp.where(start, 0.0, 1.0)[:, :, None], (2, N_CHUNKS, LANES)).astype(F32)
    lat_b = jnp.clip((c - CTX_CHUNKS) // lat_per_seq, 0, DEC_BATCH - 1)
    seed = jnp.transpose(state_l, (1, 0, 2))[:, lat_b, :]
    h0 = jnp.where((start & ~is_ctx[None, :])[:, :, None], seed, 0.0).astype(F32)
    return keep, h0


def _pool_kernel(x_ref, w_ref, sc_ref, o_ref):
    tile = pl.program_id(0)
    ext_rows = POOL_CHUNK + 2 * POOL_HALO
    is_ctx = tile * POOL_TILE < N_CTX
    erow = lax.broadcasted_iota(jnp.int32, (ext_rows, LANES), 0)
    row = lax.broadcasted_iota(jnp.int32, (POOL_CHUNK, LANES), 0)

    def shift_up(x, k):
        return x if k == 0 else pltpu.roll(x, ext_rows - k, 0)

    for g, win in enumerate(POOL_WINDOWS):
        half = win // 2
        cols = slice(g * POOL_GROUP, (g + 1) * POOL_GROUP)
        w = w_ref[g]
        sc = sc_ref[:, cols]

        def body(c, carry):
            r0 = pl.multiple_of(c * POOL_CHUNK, POOL_CHUNK)
            seq_len = jnp.where(is_ctx, SEQ, DEC_SEQ)
            t0 = jnp.where(is_ctx, 0, (tile * POOL_TILE + r0) & (DEC_SEQ - 1))
            first = t0 == 0
            last = t0 + POOL_CHUNK == seq_len
            lo = pl.multiple_of(jnp.maximum(r0 - POOL_HALO, 0), POOL_HALO)
            hi = pl.multiple_of(jnp.minimum(r0 + POOL_CHUNK, POOL_TILE - POOL_HALO), POOL_HALO)
            cur = x_ref[pl.ds(r0, POOL_CHUNK), cols]
            ext = jnp.concatenate(
                [x_ref[pl.ds(lo, POOL_HALO), cols], cur, x_ref[pl.ds(hi, POOL_HALO), cols]], axis=0)
            outside = jnp.logical_or(jnp.logical_and(first, erow < POOL_HALO),
                                     jnp.logical_and(last, erow >= POOL_HALO + POOL_CHUNK))
            acc = jnp.where(outside, 0.0, ext)
            span = 1
            while span < win:
                acc = acc + shift_up(acc, span)
                span *= 2
            total = shift_up(acc, POOL_HALO - half)[:POOL_CHUNK]
            t = t0 + row
            cnt = jnp.minimum(t + half, seq_len) - jnp.maximum(t - half, 0)
            pooled = total / cnt.astype(F32) - cur
            y = _dot(pooled.astype(BF16), w) * sc
            o_ref[pl.ds(r0, POOL_CHUNK), cols] = y.astype(o_ref.dtype)
            return carry

        lax.fori_loop(0, POOL_TILE // POOL_CHUNK, body, 0, unroll=2)


def _pool(p, pool_w_bf16, pool_scale):
    blk = COL_PU // D_POOL
    return pl.pallas_call(
        _pool_kernel,
        grid=(N_TOK // POOL_TILE,),
        in_specs=[
            pl.BlockSpec((POOL_TILE, D_POOL), lambda i: (i, blk)),
            pl.BlockSpec((len(POOL_WINDOWS), POOL_GROUP, POOL_GROUP), lambda i: (0, 0, 0)),
            pl.BlockSpec((1, D_POOL), lambda i: (0, 0)),
        ],
        out_specs=pl.BlockSpec((POOL_TILE, D_POOL), lambda i: (i, 0)),
        out_shape=jax.ShapeDtypeStruct((N_TOK, D_POOL), BF16),
        compiler_params=_params("arbitrary"),
        name="pool_mixer",
    )(p, pool_w_bf16, pool_scale)


OUTPROJ_TM = 512


def _outproj_kernel(attn_ref, lru_ref, pool_ref, *refs, two_x):
    if two_x:
        xa_ref, xb_ref, gate_ref, wa_ref, wl_ref, wp_ref, o_ref = refs
        x = jnp.where(pl.program_id(0) < N_CTX // OUTPROJ_TM, xa_ref[...], xb_ref[...])
    else:
        x_ref, gate_ref, wa_ref, wl_ref, wp_ref, o_ref = refs
        x = x_ref[...]
    mix = _dot(attn_ref[...], wa_ref[...]) + _dot(lru_ref[...], wl_ref[...]) + _dot(pool_ref[...], wp_ref[...])
    o_ref[...] = x + gate_ref[...] * mix


def _outproj(attn, lru, pool, xs, mod, w_out_bf16):
    tm = OUTPROJ_TM
    nt = N_TOK // tm
    n_ctx_tiles = N_CTX // tm
    two_x = len(xs) == 2
    if two_x:
        x_specs = [pl.BlockSpec((tm, D_MODEL), lambda i: (jnp.minimum(i, n_ctx_tiles - 1), 0)),
                   pl.BlockSpec((tm, D_MODEL), lambda i: (jnp.maximum(i - n_ctx_tiles, 0), 0))]
    else:
        x_specs = [pl.BlockSpec((tm, D_MODEL), lambda i: (i, 0))]
    return pl.pallas_call(
        functools.partial(_outproj_kernel, two_x=two_x),
        grid=(nt,),
        in_specs=[
            pl.BlockSpec((tm, D_ATTN), lambda i: (i, 0)),
            pl.BlockSpec((tm, D_LRU), lambda i: (i, 0)),
            pl.BlockSpec((tm, D_POOL), lambda i: (i, 0)),
        ] + x_specs + [
            pl.BlockSpec((None, 1, D_MODEL), lambda i: (_group_of_tile(i, tm) * N_MOD + 2, 0, 0)),
            pl.BlockSpec((D_ATTN, D_MODEL), lambda i: (0, 0)),
            pl.BlockSpec((D_LRU, D_MODEL), lambda i: (D_ATTN // D_LRU, 0)),
            pl.BlockSpec((D_POOL, D_MODEL), lambda i: ((D_ATTN + D_LRU) // D_POOL, 0)),
        ],
        out_specs=pl.BlockSpec((tm, D_MODEL), lambda i: (i, 0)),
        out_shape=jax.ShapeDtypeStruct((N_TOK, D_MODEL), F32),
        compiler_params=_params("arbitrary"),
        name="outproj",
    )(attn, lru, pool, *xs, mod, w_out_bf16, w_out_bf16, w_out_bf16)


FFN_TM = 1024
FFN_TF = 512
FFN_TN = 256
FFN_LAST_ROW_GROUPS = 4
FFN_CAST_COLS = 2 * FFN_TF
FFN_PIECE = 128
FFN_PIECES = FFN_TM // FFN_PIECE


def _ffn_kernel(xn_ref, xc_ref, g_ref, shift_ref, scale_ref, gate_ref, wg_ref, wu_ref, w2_ref, fg_ref, *rest,
                final_norm, cast_next):
    if cast_next:
        w1n_f32_ref, w2n_f32_ref, o_ref, w1n_ref, w2n_ref, h0_ref, h1_ref = rest
    else:
        o_ref, h0_ref, h1_ref = rest
    i, j = pl.program_id(0), pl.program_id(1)
    odd = i % 2 == 1
    row0 = pl.multiple_of(jnp.minimum(j, FFN_PIECES - 1) * FFN_PIECE, FFN_PIECE)
    gain = g_ref[...] * (1.0 + scale_ref[...])
    shift = shift_ref[...]
    once = jnp.where(j < FFN_PIECES, 1.0, 0.0)

    def norm_piece(h_write):
        for c in range(FFN_PIECE // NORM_ROWS):
            x = xn_ref[c * NORM_ROWS:(c + 1) * NORM_ROWS, :]
            ms = jnp.mean(x * x, axis=-1, keepdims=True)
            rows = pl.ds(pl.multiple_of(row0 + c * NORM_ROWS, NORM_ROWS), NORM_ROWS)
            h_write[rows, :] = (x * lax.rsqrt(ms + RMS_EPS) * gain + shift).astype(BF16)

    def step(h_write, h_read):
        norm_piece(h_write)
        h = h_read[...]
        act = (_silu(_dot(h, wg_ref[...])) * _dot(h, wu_ref[...])).astype(BF16)
        o_ref[pl.ds(row0, FFN_PIECE), :] += once * xc_ref[...]
        if cast_next:
            w1n_ref[...] = w1n_f32_ref[...].astype(BF16)
            w2n_ref[...] = w2n_f32_ref[...].astype(BF16)
        n_chunks = D_MODEL // FFN_TN
        for n in range(n_chunks):
            cols = slice(n * FFN_TN, (n + 1) * FFN_TN)
            row_groups = FFN_LAST_ROW_GROUPS if n == n_chunks - 1 else 1
            rg = FFN_TM // row_groups
            for r in range(row_groups):
                rows = slice(r * rg, (r + 1) * rg)
                o_ref[rows, cols] += gate_ref[:, cols] * _dot(act[rows], w2_ref[:, cols])

    @pl.when(i == 0)
    def _():
        norm_piece(h0_ref)

    @pl.when(jnp.logical_and(i > 0, j == 0))
    def _():
        o_ref[...] = jnp.zeros_like(o_ref)

    @pl.when(odd)
    def _():
        step(h1_ref, h0_ref)

    @pl.when(jnp.logical_and(i > 0, jnp.logical_not(odd)))
    def _():
        step(h0_ref, h1_ref)

    if final_norm:
        @pl.when(jnp.logical_and(i > 0, j == pl.num_programs(1) - 1))
        def _():
            fgain = fg_ref[...]

            def body(r, carry):
                r0 = pl.multiple_of(r * NORM_ROWS, NORM_ROWS)
                x = o_ref[pl.ds(r0, NORM_ROWS), :]
                ms = jnp.mean(x * x, axis=-1, keepdims=True)
                o_ref[pl.ds(r0, NORM_ROWS), :] = x * lax.rsqrt(ms + RMS_EPS) * fgain
                return carry

            lax.fori_loop(0, FFN_TM // NORM_ROWS, body, 0, unroll=NORM_UNROLL)


def _ffn(x, norm_g, mod, w1_l, w2_l, final_g, layer, tile0, ntiles, final_norm, w1_f32=None, w2_f32=None):
    tm, tf = FFN_TM, FFN_TF
    nf = D_FF // tf
    cast_next = w1_f32 is not None
    assert FFN_PIECES <= nf

    def norm_tile(i):
        return tile0 + jnp.minimum(i, ntiles - 1)

    def mm_tile(i):
        return tile0 + jnp.maximum(i - 1, 0)

    def piece(j):
        return jnp.minimum(j, FFN_PIECES - 1)

    def wcol(i, j):
        return jnp.where(i == 0, 0, j)

    def mod_spec(which, tile_fn):
        return pl.BlockSpec((None, 1, D_MODEL), lambda i, j: (_group_of_tile(tile_fn(i), tm) * N_MOD + which, 0, 0))

    in_specs = [
        pl.BlockSpec((FFN_PIECE, D_MODEL), lambda i, j: (norm_tile(i) * FFN_PIECES + piece(j), 0)),
        pl.BlockSpec((FFN_PIECE, D_MODEL), lambda i, j: (mm_tile(i) * FFN_PIECES + piece(j), 0)),
        pl.BlockSpec((1, D_MODEL), lambda i, j: (0, 0)),
        mod_spec(3, norm_tile),
        mod_spec(4, norm_tile),
        mod_spec(5, mm_tile),
        pl.BlockSpec((D_MODEL, tf), lambda i, j: (0, wcol(i, j))),
        pl.BlockSpec((D_MODEL, tf), lambda i, j: (0, nf + wcol(i, j))),
        pl.BlockSpec((tf, D_MODEL), lambda i, j: (wcol(i, j), 0)),
        pl.BlockSpec((1, D_MODEL), lambda i, j: (0, 0)),
    ]
    out_specs = [pl.BlockSpec((tm, D_MODEL), lambda i, j: (jnp.maximum(i - 1, 0), 0))]
    out_shape = [jax.ShapeDtypeStruct((ntiles * tm, D_MODEL), F32)]
    args = [x, x, norm_g, mod, mod, mod, w1_l, w1_l, w2_l, final_g]
    if cast_next:
        assert ntiles >= CAST_TILES and 2 * D_FF == nf * FFN_CAST_COLS

        def w1_idx(i, j):
            t = i - 1
            return jnp.clip(t, 0, CAST_TILES - 1), jnp.where(t < 0, 0, jnp.where(t < CAST_TILES, j, nf - 1))

        def w2_idx(i, j):
            return w1_idx(i, j)[::-1]

        nxt = layer + 1
        in_specs += [
            pl.BlockSpec((None, CAST_ROWS, FFN_CAST_COLS), lambda i, j: (nxt,) + w1_idx(i, j)),
            pl.BlockSpec((None, tf, CAST_ROWS), lambda i, j: (nxt,) + w2_idx(i, j)),
        ]
        out_specs += [pl.BlockSpec((CAST_ROWS, FFN_CAST_COLS), w1_idx), pl.BlockSpec((tf, CAST_ROWS), w2_idx)]
        out_shape += [jax.ShapeDtypeStruct((D_MODEL, 2 * D_FF), BF16), jax.ShapeDtypeStruct((D_FF, D_MODEL), BF16)]
        args += [w1_f32, w2_f32]
    out = pl.pallas_call(
        functools.partial(_ffn_kernel, final_norm=final_norm, cast_next=cast_next),
        grid=(ntiles + 1, nf),
        in_specs=in_specs,
        out_specs=out_specs,
        out_shape=out_shape,
        scratch_shapes=[pltpu.VMEM((tm, D_MODEL), BF16), pltpu.VMEM((tm, D_MODEL), BF16)],
        compiler_params=_params("arbitrary", "arbitrary"),
        name="ffn_final" if final_norm else ("ffn_cast" if cast_next else "ffn"),
    )(*args)
    return out if cast_next else out[0]


def kernel(x_prompt, x_sample, cache_k, cache_v, state_lru, c, c_ctx, mod_w, mod_b, norm_mix, norm_ffn,
           w_in, attn_sink, conv_w, conv_b, lru_wa, lru_ba, lru_wx, lru_bx, lru_lambda, pool_w,
           pool_scale, w_out, ffn_w1, ffn_w2, norm_final):
    xs = (x_prompt.reshape(N_CTX, D_MODEL), x_sample.reshape(N_LAT, D_MODEL))

    cond = jnp.zeros((COND_ROWS, D_MODEL), F32).at[0].set(c_ctx).at[1:1 + DEC_BATCH].set(c)
    mod_all = _mod_vectors(cond, mod_w, mod_b)
    mod_all = mod_all[:, :N_GROUPS].reshape(DEPTH, N_GROUPS * N_MOD, 1, D_MODEL)

    w_in_l = w_in[0].astype(BF16)
    w_out_l = w_out[0].astype(BF16)
    pool_w_b = pool_w.astype(BF16)
    w4 = (0.5 * jnp.concatenate([lru_wa[:, 0], lru_wx[:, 0], lru_wa[:, 1], lru_wx[:, 1]], axis=-1)).astype(BF16)
    ba = lru_ba.reshape(DEPTH, 2, LRU_BLOCKS, LRU_BLK)
    bx = lru_bx.reshape(DEPTH, 2, LRU_BLOCKS, LRU_BLK)
    b4 = 0.5 * jnp.stack([ba[:, 0], bx[:, 0], ba[:, 1], bx[:, 1]], axis=2).reshape(DEPTH, LRU_BLOCKS, 1, 4 * LRU_BLK)
    cache_k2 = cache_k.reshape(DEC_BATCH, DEPTH, PAST_LEN, D_KV)
    cache_v2 = cache_v.reshape(DEC_BATCH, DEPTH, PAST_LEN, D_KV)
    rope_tabs = _rope_tables()

    kc = jnp.zeros((BATCH, DEPTH, SEQ, D_KV), F32)
    vc = jnp.zeros((BATCH, DEPTH, SEQ, D_KV), F32)
    attn = jnp.zeros((N_TOK, D_ATTN), BF16)
    hs_new = []
    for l in range(DEPTH):
        mod = mod_all[l]
        last = l == DEPTH - 1
        w_out_cur = w_out_l
        if last:
            p, kc, vc = _inproj(xs, norm_mix[l][None, :], mod, w_in_l, kc, vc, l)
        else:
            p, kc, vc, w_in_l, w_out_l = _inproj(xs, norm_mix[l][None, :], mod, w_in_l, kc, vc, l, w_in, w_out)

        if l == 0:
            attn, w2_l = _ctx_attention(p, attn_sink[l], attn, ffn_w2)
            attn, w1_l = _lat_attention(p, attn_sink[l], cache_k2, cache_v2, rope_tabs, attn, l, ffn_w1)
        else:
            attn = _ctx_attention(p, attn_sink[l], attn)
            attn = _lat_attention(p, attn_sink[l], cache_k2, cache_v2, rope_tabs, attn, l)
        keep, h0 = _lru_chain_inputs(state_lru[:, l])
        lru, hend = _lru(p, conv_w[l], conv_b[l][None, :], w4[l], b4[l], lru_lambda[l], keep, h0)
        pool = _pool(p, pool_w_b[l], pool_scale[l][None, :])

        ctx_per_seq = SEQ // CHUNK
        fwd_fin = hend[0, ctx_per_seq - 1:CTX_CHUNKS:ctx_per_seq]
        bwd_fin = hend[1, 0:CTX_CHUNKS:ctx_per_seq]
        hs_new.append(jnp.stack([fwd_fin, bwd_fin], axis=1))

        x = _outproj(attn, lru, pool, xs, mod, w_out_cur)
        ffn_args = (x, norm_ffn[l][None, :], mod, w1_l, w2_l, norm_final[None, :], l)
        if not last:
            x, w1_l, w2_l = _ffn(*ffn_args, 0, N_TOK // FFN_TM, False, ffn_w1, ffn_w2)
            xs = (x,)

    y_prompt = _ffn(*ffn_args, 0, N_CTX // FFN_TM, True).reshape(BATCH, SEQ, D_MODEL)
    y_sample = _ffn(*ffn_args, N_CTX // FFN_TM, N_LAT // FFN_TM, True).reshape(DEC_BATCH, DEC_SEQ, D_MODEL)
    cache_shape = (BATCH, DEPTH, SEQ, N_KV_HEADS, HEAD_DIM)
    return (y_prompt, y_sample, kc.reshape(cache_shape), vc.reshape(cache_shape), jnp.stack(hs_new, axis=1))
```
